```python
import math
import jax
import jax.numpy as jnp
from jax import lax
import numpy as np

D_MODEL = 1024
BATCH = 16
SEQ = 256
DEPTH = 4
DEC_BATCH = 8
DEC_SEQ = 1024
PAST_LEN = 256

GRID_W = 64
N_MIXERS = 2
N_GDN = (DEPTH + 1) // 2
N_NA = DEPTH // 2
GDN_HEADS = 8
GDN_DK = 128
GDN_DV = 128
GDN_QK_W = GDN_HEADS * GDN_DK
GDN_W = GDN_HEADS * GDN_DV
GDN_CONV = 3
GDN_CHUNK = 64
NA_HEADS = 16
NA_DH = D_MODEL // NA_HEADS
NA_W = NA_HEADS * NA_DH
NA_KR_MAX = 8
NA_KC = 16
CTX_QBLOCK = 128
D_FF = 2816
FFN_CONV = 3
N_MOD = 6
EPS = 1e-6
NEG_INF = -1e30

kernel_name = 'hybrid_gdn_natten_diffusion_step'


def rmsnorm(x, g):
    xf = x.astype(jnp.float32)
    y = xf * lax.rsqrt(jnp.mean(xf * xf, axis=-1, keepdims=True) + EPS)
    return y.astype(x.dtype) * g


def l2norm(x):
    xf = x.astype(jnp.float32)
    return xf * lax.rsqrt(jnp.sum(xf * xf, axis=-1, keepdims=True) + EPS)


def dwconv_centred(x, w):
    k, ch = w.shape
    p = k // 2
    return lax.conv_general_dilated(x, w.reshape(k, 1, ch).astype(x.dtype), window_strides=(1,),
                                    padding=((p, p),), dimension_numbers=('NWC', 'WIO', 'NWC'),
                                    feature_group_count=ch)


def modulation(cvec, w, b):
    m = jax.nn.silu(cvec) @ w + b
    return [t[:, None, :] for t in jnp.split(m, N_MOD, axis=-1)]


def chunk_gated_delta(q, k, v, g, beta, s0):
    f32 = jnp.float32
    B, T, H, DK = q.shape
    DV = v.shape[-1]
    C = GDN_CHUNK
    N = T // C

    def chunks(t):
        t = t.astype(f32).reshape((B, N, C, H) + t.shape[3:])
        return jnp.moveaxis(t, (1, 3), (0, 2))

    qc = chunks(q) * DK ** -0.5
    kc = chunks(k)
    vc = chunks(v)
    gc = jnp.cumsum(chunks(g), axis=-1)
    bc = chunks(beta)
    idx = jnp.arange(C)
    lower = idx[:, None] >= idx[None, :]
    strict = idx[:, None] > idx[None, :]
    diff = gc[..., :, None] - gc[..., None, :]
    decay = jnp.where(lower, jnp.exp(jnp.where(lower, diff, 0.0)), 0.0)
    kb = kc * bc[..., None]
    a_mat = jnp.where(strict, jnp.einsum('nbhid,nbhjd->nbhij', kb, kc) * decay, 0.0)
    eye = jnp.broadcast_to(jnp.eye(C, dtype=f32), a_mat.shape)
    t_mat = lax.linalg.triangular_solve(eye + a_mat, eye, left_side=True, lower=True)
    u = jnp.einsum('nbhij,nbhjd->nbhid', t_mat, vc * bc[..., None])
    w = jnp.einsum('nbhij,nbhjd->nbhid', t_mat, kb * jnp.exp(gc)[..., None])

    def step(s, inp):
        qi, ki, ui, wi, gi, di = inp
        v_new = ui - jnp.einsum('bhcd,bhde->bhce', wi, s)
        intra = jnp.einsum('bhcd,bhjd->bhcj', qi, ki) * di
        o = (jnp.einsum('bhcd,bhde->bhce', qi * jnp.exp(gi)[..., None], s)
             + jnp.einsum('bhcj,bhje->bhce', intra, v_new))
        g_last = gi[..., -1]
        s = (s * jnp.exp(g_last)[..., None, None]
             + jnp.einsum('bhcd,bhce->bhde', ki * jnp.exp(g_last[..., None] - gi)[..., None], v_new))
        return s, o

    s_fin, o = lax.scan(step, s0.astype(f32), (qc, kc, u, w, gc, decay))
    o = jnp.moveaxis(o, (0, 2), (1, 3)).reshape(B, T, H, DV)
    return o, s_fin.astype(s0.dtype)


def gdn_mixer(h, s0, w_in, conv_w, a_log, dt_bias, norm_g, w_out):
    B, T, _ = h.shape
    proj = h @ w_in
    qkv, z, a, b = jnp.split(proj, [2 * GDN_QK_W + GDN_W, 2 * GDN_QK_W + 2 * GDN_W,
                                    2 * GDN_QK_W + 2 * GDN_W + 2 * GDN_HEADS], axis=-1)
    qkv = jax.nn.silu(dwconv_centred(qkv, conv_w))
    q, k, v = jnp.split(qkv, [GDN_QK_W, 2 * GDN_QK_W], axis=-1)
    q = l2norm(q.reshape(B, T, GDN_HEADS, GDN_DK))
    k = l2norm(k.reshape(B, T, GDN_HEADS, GDN_DK))
    v = v.reshape(B, T, GDN_HEADS, GDN_DV)
    a = a.reshape(B, T, 2, GDN_HEADS).astype(jnp.float32)
    b = b.reshape(B, T, 2, GDN_HEADS).astype(jnp.float32)
    g = -jnp.exp(a_log.astype(jnp.float32)) * jax.nn.softplus(a + dt_bias.astype(jnp.float32))
    beta = jax.nn.sigmoid(b)
    o_f, s_f = chunk_gated_delta(q, k, v, g[:, :, 0], beta[:, :, 0], s0[:, 0])
    rev = lambda t: jnp.flip(t, axis=1)
    o_b, s_b = chunk_gated_delta(rev(q), rev(k), rev(v), rev(g[:, :, 1]), rev(beta[:, :, 1]), s0[:, 1])
    o = (o_f + rev(o_b)).astype(h.dtype)
    o = rmsnorm(o, norm_g) * jax.nn.silu(z.reshape(B, T, GDN_HEADS, GDN_DV))
    return o.reshape(B, T, GDN_W) @ w_out, jnp.stack([s_f, s_b], axis=1)


def na_project(h, w_qkv):
    B, T, _ = h.shape
    q, k, v = jnp.split(h @ w_qkv, 3, axis=-1)
    shp = (B, T, NA_HEADS, NA_DH)
    return q.reshape(shp), k.reshape(shp), v.reshape(shp)


def context_attention(q, k, v):
    B, T, H, Dh = q.shape
    nb = T // CTX_QBLOCK
    qb = jnp.moveaxis(q.reshape(B, nb, CTX_QBLOCK, H, Dh), 1, 0)

    def block(qi):
        s = jnp.einsum('bqhd,bkhd->bhqk', qi, k).astype(jnp.float32) * Dh ** -0.5
        p = jax.nn.softmax(s, axis=-1).astype(v.dtype)
        return jnp.einsum('bhqk,bkhd->bqhd', p, v)

    o = lax.map(block, qb)
    return jnp.moveaxis(o, 0, 1).reshape(B, T, H * Dh)


def neighbourhood_attention(q, k, v, k_ctx, v_ctx, rel_bias):
    B, T, H, Dh = q.shape
    rows = T // GRID_W
    kr = min(NA_KR_MAX, rows)
    scale = Dh ** -0.5
    qg = q.reshape(B, rows, GRID_W, H, Dh)
    kg = k.reshape(B, rows, GRID_W, H, Dh)
    vg = v.reshape(B, rows, GRID_W, H, Dh)
    col = jnp.arange(GRID_W)
    col_start = jnp.clip(col - NA_KC // 2, 0, GRID_W - NA_KC)
    col_mask = (col[None, :] >= col_start[:, None]) & (col[None, :] < col_start[:, None] + NA_KC)
    col_idx = jnp.clip(col[None, :] - col[:, None] + NA_KC - 1, 0, 2 * NA_KC - 2)

    def row_block(args):
        r, q_r = args
        rs = jnp.clip(r - kr // 2, 0, rows - kr)
        k_blk = lax.dynamic_slice_in_dim(kg, rs, kr, axis=1)
        v_blk = lax.dynamic_slice_in_dim(vg, rs, kr, axis=1)
        row_idx = rs + jnp.arange(kr) - r + NA_KR_MAX - 1
        bias = rel_bias[:, row_idx][:, :, col_idx]
        s_loc = (jnp.einsum('bqhd,brkhd->bhqrk', q_r, k_blk).astype(jnp.float32) * scale
                 + jnp.transpose(bias, (0, 2, 1, 3))[None].astype(jnp.float32))
        s_loc = jnp.where(col_mask[:, None, :], s_loc, NEG_INF)
        s_ctx = jnp.einsum('bqhd,bchd->bhqc', q_r, k_ctx).astype(jnp.float32) * scale
        p = jax.nn.softmax(jnp.concatenate([s_loc.reshape(B, H, GRID_W, kr * GRID_W), s_ctx], axis=-1),
                           axis=-1).astype(v.dtype)
        p_loc = p[..., :kr * GRID_W].reshape(B, H, GRID_W, kr, GRID_W)
        p_ctx = p[..., kr * GRID_W:]
        return (jnp.einsum('bhqrk,brkhd->bqhd', p_loc, v_blk)
                + jnp.einsum('bhqc,bchd->bqhd', p_ctx, v_ctx))

    o = lax.map(row_block, (jnp.arange(rows), jnp.moveaxis(qg, 1, 0)))
    return jnp.moveaxis(o, 0, 1).reshape(B, T, H * Dh)


def conv_ffn(h, w_up, conv_w, conv_b, w_down):
    u = dwconv_centred(h @ w_up, conv_w) + conv_b
    val, gate = jnp.split(u, 2, axis=-1)
    return (jax.nn.silu(gate) * val) @ w_down


def setup_inputs(seed: int = 0) -> dict:
    key = jax.random.key(seed)
    ks = jax.random.split(key, 26)
    f32 = jnp.float32

    def nrm(k, shape, scale):
        return jax.random.normal(k, shape, f32) * scale

    d_in = 2 * GDN_QK_W + 2 * GDN_W + 4 * GDN_HEADS
    dt = jnp.exp(jax.random.uniform(ks[13], (N_GDN, 2, GDN_HEADS), f32, math.log(1e-3), math.log(1e-1)))
    return {
        'x_prompt': nrm(ks[0], (BATCH, SEQ, D_MODEL), 1.0),
        'x_sample': nrm(ks[1], (DEC_BATCH, DEC_SEQ, D_MODEL), 1.0),
        'state_gdn': nrm(ks[2], (DEC_BATCH, N_GDN, 2, GDN_HEADS, GDN_DK, GDN_DV), 0.3),
        'cache_k': nrm(ks[3], (DEC_BATCH, N_NA, PAST_LEN, NA_HEADS, NA_DH), 1.0),
        'cache_v': nrm(ks[4], (DEC_BATCH, N_NA, PAST_LEN, NA_HEADS, NA_DH), 1.0),
        'c': nrm(ks[5], (DEC_BATCH, D_MODEL), 1.0),
        'c_ctx': nrm(ks[6], (D_MODEL,), 1.0),
        'w_ada': nrm(ks[7], (DEPTH, D_MODEL, N_MOD * D_MODEL), 0.5 * D_MODEL ** -0.5),
        'b_ada': nrm(ks[8], (DEPTH, N_MOD * D_MODEL), 0.02),
        'norm1_g': 1.0 + nrm(ks[9], (DEPTH, D_MODEL), 0.02),
        'norm2_g': 1.0 + nrm(ks[10], (DEPTH, D_MODEL), 0.02),
        'gdn_w_in': nrm(ks[11], (N_GDN, D_MODEL, d_in), D_MODEL ** -0.5),
        'gdn_conv_w': nrm(ks[12], (N_GDN, GDN_CONV, 2 * GDN_QK_W + GDN_W), GDN_CONV ** -0.5),
        'gdn_a_log': jnp.log(jax.random.uniform(ks[14], (N_GDN, 2, GDN_HEADS), f32, 1.0, 16.0)),
        'gdn_dt_bias': dt + jnp.log(-jnp.expm1(-dt)),
        'gdn_norm_g': 1.0 + nrm(ks[15], (N_GDN, GDN_DV), 0.02),
        'gdn_w_out': nrm(ks[16], (N_GDN, GDN_W, D_MODEL), GDN_W ** -0.5),
        'na_w_qkv': nrm(ks[17], (N_NA, D_MODEL, 3 * NA_W), D_MODEL ** -0.5),
        'na_rel_bias': nrm(ks[18], (N_NA, NA_HEADS, 2 * NA_KR_MAX - 1, 2 * NA_KC - 1), 0.1),
        'na_w_out': nrm(ks[19], (N_NA, NA_W, D_MODEL), NA_W ** -0.5),
        'ffn_w_up': nrm(ks[20], (DEPTH, D_MODEL, 2 * D_FF), D_MODEL ** -0.5),
        'ffn_conv_w': nrm(ks[21], (DEPTH, FFN_CONV, 2 * D_FF), FFN_CONV ** -0.5),
        'ffn_conv_b': nrm(ks[22], (DEPTH, 2 * D_FF), 0.02),
        'ffn_w_down': nrm(ks[23], (DEPTH, D_FF, D_MODEL), D_FF ** -0.5),
        'final_g': 1.0 + nrm(ks[24], (D_MODEL,), 0.02),
    }


def reference(x_prompt, x_sample, state_gdn, cache_k, cache_v, c, c_ctx, w_ada, b_ada, norm1_g, norm2_g,
              gdn_w_in, gdn_conv_w, gdn_a_log, gdn_dt_bias, gdn_norm_g, gdn_w_out,
              na_w_qkv, na_rel_bias, na_w_out, ffn_w_up, ffn_conv_w, ffn_conv_b, ffn_w_down, final_g):
    xp, xs = x_prompt, x_sample
    gdn_states, na_keys, na_vals = [], [], []
    for l in range(DEPTH):
        j = l // N_MIXERS
        sh1p, sc1p, g1p, sh2p, sc2p, g2p = modulation(c_ctx[None, :], w_ada[l], b_ada[l])
        sh1s, sc1s, g1s, sh2s, sc2s, g2s = modulation(c, w_ada[l], b_ada[l])
        hp = rmsnorm(xp, norm1_g[l]) * (1.0 + sc1p) + sh1p
        hs = rmsnorm(xs, norm1_g[l]) * (1.0 + sc1s) + sh1s
        if l % N_MIXERS == 0:
            zero_state = jnp.zeros((xp.shape[0], 2, GDN_HEADS, GDN_DK, GDN_DV), xp.dtype)
            mp, st = gdn_mixer(hp, zero_state, gdn_w_in[j], gdn_conv_w[j], gdn_a_log[j], gdn_dt_bias[j],
                               gdn_norm_g[j], gdn_w_out[j])
            ms, _ = gdn_mixer(hs, state_gdn[:, j], gdn_w_in[j], gdn_conv_w[j], gdn_a_log[j], gdn_dt_bias[j],
                              gdn_norm_g[j], gdn_w_out[j])
            gdn_states.append(st)
        else:
            qp, kp, vp = na_project(hp, na_w_qkv[j])
            mp = context_attention(qp, kp, vp) @ na_w_out[j]
            qs, kls, vls = na_project(hs, na_w_qkv[j])
            ms = neighbourhood_attention(qs, kls, vls, cache_k[:, j], cache_v[:, j], na_rel_bias[j]) @ na_w_out[j]
            na_keys.append(kp)
            na_vals.append(vp)
        xp = xp + g1p * mp
        xs = xs + g1s * ms
        xp = xp + g2p * conv_ffn(rmsnorm(xp, norm2_g[l]) * (1.0 + sc2p) + sh2p,
                                 ffn_w_up[l], ffn_conv_w[l], ffn_conv_b[l], ffn_w_down[l])
        xs = xs + g2s * conv_ffn(rmsnorm(xs, norm2_g[l]) * (1.0 + sc2s) + sh2s,
                                 ffn_w_up[l], ffn_conv_w[l], ffn_conv_b[l], ffn_w_down[l])
    y_prompt = rmsnorm(xp, final_g)
    y_sample = rmsnorm(xs, final_g)
    new_state_gdn = jnp.stack(gdn_states, axis=1)
    new_cache_k = jnp.stack(na_keys, axis=1)
    new_cache_v = jnp.stack(na_vals, axis=1)
    return (y_prompt, y_sample, new_state_gdn, new_cache_k, new_cache_v)
```

```python
import functools
import math

import jax
import jax.numpy as jnp
import numpy as np
from jax import lax
from jax.experimental import pallas as pl
from jax.experimental.pallas import tpu as pltpu

f32 = jnp.float32
bf16 = jnp.bfloat16

D_MODEL = 1024
BATCH = 16
SEQ = 256
DEPTH = 4
DEC_BATCH = 8
DEC_SEQ = 1024
PAST_LEN = 256
GRID_W = 64
GRID_ROWS = DEC_SEQ // GRID_W
GDN_HEADS = 8
GDN_DK = 128
GDN_CHUNK = 64
NA_HEADS = 16
NA_DH = 64
NA_KR = 8
NA_KC = 16
D_FF = 2816
N_MOD = 6
EPS = 1e-6
NEG_INF = -1e30

N_PROMPT = BATCH * SEQ
N_SAMPLE = DEC_BATCH * DEC_SEQ
N_TOK = N_PROMPT + N_SAMPLE

TM = 1024
TP = 512
HALO = 16
FF_CHUNK = 256
N_FF_CHUNKS = D_FF // FF_CHUNK
HEAD_GROUP = 4
N_HEAD_GROUPS = GDN_HEADS // HEAD_GROUP
GATE_LANES = 128
MOD_ROWS = 16
MOD_TN = 1536

VMEM_LIMIT = 56 * 1024 * 1024


def _silu(x):
    return x * jax.nn.sigmoid(x)


def _dot(a, b):
    return jnp.dot(a, b, preferred_element_type=f32)


def _dot_nt(a, b):
    return lax.dot_general(a, b, (((1,), (1,)), ((), ())), preferred_element_type=f32)


def _dot_tn(a, b):
    return lax.dot_general(a, b, (((0,), (0,)), ((), ())), preferred_element_type=f32)


def _rms(x):
    return x * lax.rsqrt(jnp.mean(x * x, axis=-1, keepdims=True) + EPS)


def _params(sem, vmem=VMEM_LIMIT):
    return pltpu.CompilerParams(dimension_semantics=sem, vmem_limit_bytes=vmem)


def _mod_kernel(cv_ref, w_ref, b_ref, o_ref):
    s = _silu(cv_ref[...]).astype(bf16)
    o_ref[0] = _dot(s, w_ref[0].astype(bf16)) + b_ref[0]


def _modulation(cvec, w_ada, b_ada):
    n_tiles = (N_MOD * D_MODEL) // MOD_TN
    return pl.pallas_call(
        _mod_kernel,
        grid=(DEPTH, n_tiles),
        in_specs=[
            pl.BlockSpec((MOD_ROWS, D_MODEL), lambda l, n: (0, 0)),
            pl.BlockSpec((1, D_MODEL, MOD_TN), lambda l, n: (l, 0, n)),
            pl.BlockSpec((1, 1, MOD_TN), lambda l, n: (l, 0, n)),
        ],
        out_specs=pl.BlockSpec((1, MOD_ROWS, MOD_TN), lambda l, n: (l, 0, n)),
        out_shape=jax.ShapeDtypeStruct((DEPTH, MOD_ROWS, N_MOD * D_MODEL), f32),
        compiler_params=_params(("arbitrary", "arbitrary")),
        name="modulation",
    )(cvec, w_ada, b_ada.reshape(DEPTH, 1, N_MOD * D_MODEL))


def _mod_row_tm(i):
    return jnp.maximum(i - (N_PROMPT // TM - 1), 0)


def _norm_mod(x, g, shift, scale):
    return (_rms(x) * g) * (1.0 + scale) + shift


def _na_in_kernel(x_ref, mod_ref, g_ref, w_ref, o_ref, h_ref):
    @pl.when(pl.program_id(1) == 0)
    def _():
        m = mod_ref[0, 0]
        h_ref[...] = _norm_mod(x_ref[...], g_ref[0], m[0:1], m[1:2]).astype(bf16)

    o_ref[...] = _dot(h_ref[...], w_ref[...])


def _na_in(x, mod, norm_g, w, layer):
    n_out = w.shape[1] // D_MODEL
    return pl.pallas_call(
        _na_in_kernel,
        grid=(N_TOK // TM, n_out),
        in_specs=[
            pl.BlockSpec((TM, D_MODEL), lambda i, n: (i, 0)),
            pl.BlockSpec((1, 1, N_MOD, D_MODEL), lambda i, n: (layer, _mod_row_tm(i), 0, 0)),
            pl.BlockSpec((1, 1, D_MODEL), lambda i, n: (layer, 0, 0)),
            pl.BlockSpec((D_MODEL, D_MODEL), lambda i, n: (0, n)),
        ],
        out_specs=pl.BlockSpec((TM, D_MODEL), lambda i, n: (i, n)),
        out_shape=jax.ShapeDtypeStruct((N_TOK, w.shape[1]), f32),
        scratch_shapes=[pltpu.VMEM((TM, D_MODEL), bf16)],
        compiler_params=_params(("arbitrary", "arbitrary")),
        name="na_in",
    )(x, mod, norm_g, w)


def _seq_len_of_tile(i, tile):
    return jnp.where(i < N_PROMPT // tile, SEQ, DEC_SEQ)


def _gdn_in_kernel(x_ref, mod_ref, g_ref, w_ref, wab_ref, cw_ref, alog_ref, dtb_ref,
                   o_ref, gcol_ref, grow_ref, h_ref):
    i = pl.program_id(0)
    n = pl.program_id(1)

    @pl.when(n == 0)
    def _():
        m = mod_ref[0, 0]
        h_ref[...] = _norm_mod(x_ref[...], g_ref[0], m[0:1], m[1:2]).astype(bf16)

    def conv_silu():
        p = _dot(h_ref[...], w_ref[...])
        seq = _seq_len_of_tile(i, TM)
        pos = lax.broadcasted_iota(jnp.int32, (TM, 1), 0) & (seq - 1)
        prev = jnp.where(pos == 0, 0.0, pltpu.roll(p, 1, axis=0))
        nxt = jnp.where(pos == seq - 1, 0.0, pltpu.roll(p, TM - 1, axis=0))
        cw = cw_ref[...]
        return _silu(cw[0:1] * prev + cw[1:2] * p + cw[2:3] * nxt)

    @pl.when(n < 2)
    def _():
        s = conv_silu()
        qscale = jnp.where(n == 0, GDN_DK ** -0.5, 1.0).astype(f32)
        for h in range(GDN_HEADS):
            sh = s[:, h * GDN_DK:(h + 1) * GDN_DK]
            nh = sh * lax.rsqrt(jnp.sum(sh * sh, axis=-1, keepdims=True) + EPS)
            o_ref[:, h * GDN_DK:(h + 1) * GDN_DK] = nh * qscale

    @pl.when(n == 2)
    def _():
        o_ref[...] = conv_silu()

    @pl.when(n == 3)
    def _():
        o_ref[...] = _dot(h_ref[...], w_ref[...])

    @pl.when(n == 4)
    def _():
        ab = _dot(h_ref[...], wab_ref[...])
        width = ab.shape[1]
        kind = lax.broadcasted_iota(jnp.int32, (1, width), 1) & 7
        xs = ab + dtb_ref[...]
        softplus = jnp.maximum(xs, 0.0) + jnp.log1p(jnp.exp(-jnp.abs(xs)))
        g = -jnp.exp(alog_ref[...]) * softplus
        pos = lax.broadcasted_iota(jnp.int32, (TM, 1), 0) & (GDN_CHUNK - 1)
        cf = g
        cb = g
        s = 1
        while s < GDN_CHUNK:
            cf = cf + jnp.where(pos >= s, pltpu.roll(cf, s, axis=0), 0.0)
            cb = cb + jnp.where(pos < GDN_CHUNK - s, pltpu.roll(cb, TM - s, axis=0), 0.0)
            s *= 2
        gates = jnp.where(kind == 0, cf, jnp.where(kind == 1, cb, jnp.where(kind < 4, jax.nn.sigmoid(ab), 0.0)))
        gcol_ref[...] = gates
        for c in range(TM // GDN_CHUNK):
            for hg in range(width // GATE_LANES):
                blk = gates[c * GDN_CHUNK:(c + 1) * GDN_CHUNK, hg * GATE_LANES:(hg + 1) * GATE_LANES]
                grow_ref[c, hg * GATE_LANES:(hg + 1) * GATE_LANES, :] = blk.T


def _gdn_in(x, mod, norm_g, w_qkvz, w_ab, conv_w, alog_row, dtb_row, layer):
    n_chunks = N_TOK // GDN_CHUNK
    gw = N_HEAD_GROUPS * GATE_LANES
    last = 4 * 1 - 1
    return pl.pallas_call(
        _gdn_in_kernel,
        grid=(N_TOK // TM, 5),
        in_specs=[
            pl.BlockSpec((TM, D_MODEL), lambda i, n: (i, 0)),
            pl.BlockSpec((1, 1, N_MOD, D_MODEL), lambda i, n: (layer, _mod_row_tm(i), 0, 0)),
            pl.BlockSpec((1, 1, D_MODEL), lambda i, n: (layer, 0, 0)),
            pl.BlockSpec((D_MODEL, D_MODEL), lambda i, n: (0, jnp.minimum(n, last))),
            pl.BlockSpec((D_MODEL, gw), lambda i, n: (0, 0)),
            pl.BlockSpec((3, D_MODEL), lambda i, n: (0, jnp.minimum(n, 2))),
            pl.BlockSpec((1, gw), lambda i, n: (0, 0)),
            pl.BlockSpec((1, gw), lambda i, n: (0, 0)),
        ],
        out_specs=[
            pl.BlockSpec((TM, D_MODEL), lambda i, n: (i, jnp.minimum(n, last))),
            pl.BlockSpec((TM, gw), lambda i, n: (i, 0)),
            pl.BlockSpec((TM // GDN_CHUNK, gw, GDN_CHUNK), lambda i, n: (i, 0, 0)),
        ],
        out_shape=[
            jax.ShapeDtypeStruct((N_TOK, 4 * D_MODEL), f32),
            jax.ShapeDtypeStruct((N_TOK, gw), f32),
            jax.ShapeDtypeStruct((n_chunks, gw, GDN_CHUNK), f32),
        ],
        scratch_shapes=[pltpu.VMEM((TM, D_MODEL), bf16)],
        compiler_params=_params(("arbitrary", "arbitrary")),
        name="gdn_in",
    )(x, mod, norm_g, w_qkvz, w_ab, conv_w, alog_row, dtb_row)


def _tri_inverse(a, eye, level_masks):
    t = eye - jnp.where(level_masks[0], a, 0.0)
    for mask in level_masks[1:]:
        t16 = t.astype(bf16)
        tl = _dot(t16, jnp.where(mask, a, 0.0).astype(bf16))
        t = t - _dot(tl.astype(bf16), t16)
    return t


def _gdn_core_kernel(*refs, seq, has_s0, emit_state):
    q_ref, k_ref, v_ref, z_ref, gcol_ref, grow_ref, ng_ref = refs[:7]
    pos = 7
    s0_ref = None
    if has_s0:
        s0_ref = refs[pos]
        pos += 1
    pos += 1
    o_ref = refs[pos]
    pos += 1
    sfin_ref = None
    if emit_state:
        sfin_ref = refs[pos]
        pos += 1
    u_s, wq_s, kt_s, in_s, eg_s, st_s, oo_s = refs[pos:]

    C = GDN_CHUNK
    n_chunks = seq // C
    ii = lax.broadcasted_iota(jnp.int32, (C, C), 0)
    jj = lax.broadcasted_iota(jnp.int32, (C, C), 1)
    eye = (ii == jj).astype(f32)
    incl = (ii >= jj, ii <= jj)
    strict = (ii > jj, ii < jj)
    level_masks = []
    sh = 0
    while (1 << sh) < C:
        level_masks.append(((ii >> (sh + 1)) == (jj >> (sh + 1))) & ((ii >> sh) != (jj >> sh)))
        sh += 1

    def prep(c, carry):
        rows = pl.ds(pl.multiple_of(c * C, C), C)
        gc_tile = gcol_ref[rows, :]
        gr_tile = grow_ref[c]
        for hl in range(HEAD_GROUP):
            ls = slice(hl * GDN_DK, (hl + 1) * GDN_DK)
            kc = k_ref[rows, ls]
            qc = q_ref[rows, ls]
            vc = v_ref[rows, ls]
            k16 = kc.astype(bf16)
            gram = _dot_nt(k16, k16)
            qk = _dot_nt(qc.astype(bf16), k16)
            for d in range(2):
                ch = hl * 2 + d
                lane = 8 * hl + d
                gcol = gc_tile[:, lane:lane + 1]
                bcol = gc_tile[:, lane + 2:lane + 3]
                grow = gr_tile[lane:lane + 1, :]
                diff = gcol - grow
                decay = jnp.where(incl[d], jnp.exp(jnp.where(incl[d], diff, 0.0)), 0.0)
                a = jnp.where(strict[d], gram * decay * bcol, 0.0)
                t16 = _tri_inverse(a, eye, level_masks).astype(bf16)
                egc = jnp.exp(gcol)
                u = _dot(t16, (vc * bcol).astype(bf16))
                w = _dot(t16, (kc * (bcol * egc)).astype(bf16))
                glast = gcol[C - 1:C] if d == 0 else gcol[0:1]
                u_s[ch, rows, :] = u
                wq_s[ch, c, 0:C, :] = w.astype(bf16)
                wq_s[ch, c, C:2 * C, :] = (qc * egc).astype(bf16)
                kt_s[ch, rows, :] = (kc * jnp.exp(glast - gcol)).astype(bf16)
                in_s[ch, c] = (qk * decay).astype(bf16)
                eg_s[ch, c] = jnp.broadcast_to(jnp.exp(glast), (8, GDN_DK))
        return carry

    lax.fori_loop(0, n_chunks, prep, 0)

    for hl in range(HEAD_GROUP):
        for d in range(2):
            st_s[hl * 2 + d] = s0_ref[0, 0, d, hl] if has_s0 else jnp.zeros((GDN_DK, GDN_DK), f32)

    def scan(t, carry):
        for hl in range(HEAD_GROUP):
            ls = slice(hl * GDN_DK, (hl + 1) * GDN_DK)
            for d in range(2):
                ch = hl * 2 + d
                c = t if d == 0 else n_chunks - 1 - t
                rows = pl.ds(pl.multiple_of(c * C, C), C)
                s = st_s[ch]
                s16 = s.astype(bf16)
                ws_qs = _dot(wq_s[ch, c], s16)
                v16 = (u_s[ch, rows, :] - ws_qs[:C]).astype(bf16)
                oo_s[d, rows, ls] = ws_qs[C:] + _dot(in_s[ch, c], v16)
                st_s[ch] = s * eg_s[ch, c][0:1, :] + _dot_tn(kt_s[ch, rows, :], v16)
        return carry

    lax.fori_loop(0, n_chunks, scan, 0)

    ng = ng_ref[...]
    for hl in range(HEAD_GROUP):
        ls = slice(hl * GDN_DK, (hl + 1) * GDN_DK)
        o = oo_s[0, :, ls] + oo_s[1, :, ls]
        o_ref[:, ls] = (_rms(o) * ng * _silu(z_ref[:, ls])).astype(bf16)
        if emit_state:
            for d in range(2):
                sfin_ref[0, 0, d, hl] = st_s[hl * 2 + d]


def _gdn_core(qkvz, gcol, grow, norm_g_row, *, seq, n_seq, row0, og_prev, s0=None, s0_layer=0,
              state_prev=None, state_layer=0, emit_state=False):
    hw = HEAD_GROUP * GDN_DK
    rb0 = row0 // seq
    cb0 = row0 // GDN_CHUNK // (seq // GDN_CHUNK)
    n_chunks = seq // GDN_CHUNK
    n_ch = HEAD_GROUP * 2

    def col_spec(base):
        return pl.BlockSpec((seq, hw), lambda s, hg: (rb0 + s, base * N_HEAD_GROUPS + hg))

    in_specs = [col_spec(0), col_spec(1), col_spec(2), col_spec(3),
                pl.BlockSpec((seq, GATE_LANES), lambda s, hg: (rb0 + s, hg)),
                pl.BlockSpec((n_chunks, GATE_LANES, GDN_CHUNK), lambda s, hg: (cb0 + s, hg, 0)),
                pl.BlockSpec((1, GDN_DK), lambda s, hg: (0, 0))]
    args = [qkvz, qkvz, qkvz, qkvz, gcol, grow, norm_g_row]
    if s0 is not None:
        in_specs.append(pl.BlockSpec((1, 1, 2, HEAD_GROUP, GDN_DK, GDN_DK),
                                     lambda s, hg: (s, s0_layer, 0, hg, 0, 0)))
        args.append(s0)
    aliases = {}
    out_specs = [pl.BlockSpec((seq, hw), lambda s, hg: (rb0 + s, hg))]
    out_shape = [jax.ShapeDtypeStruct((N_TOK, D_MODEL), bf16)]
    in_specs.append(pl.BlockSpec(memory_space=pl.ANY))
    if og_prev is None:
        og_prev = jnp.zeros((8, 128), bf16)
    else:
        aliases[len(args)] = 0
    args.append(og_prev)
    if emit_state:
        out_specs.append(pl.BlockSpec((1, 1, 2, HEAD_GROUP, GDN_DK, GDN_DK),
                                      lambda s, hg: (s, state_layer, 0, hg, 0, 0)))
        out_shape.append(jax.ShapeDtypeStruct((BATCH, 2, 2, GDN_HEADS, GDN_DK, GDN_DK), f32))
        if state_prev is not None:
            in_specs.append(pl.BlockSpec(memory_space=pl.ANY))
            aliases[len(args)] = 1
            args.append(state_prev)

    def body(*refs):
        n_in = len(args)
        extra = 1 if (emit_state and state_prev is not None) else 0
        ins = refs[:n_in - extra]
        rest = refs[n_in:]
        _gdn_core_kernel(*ins, *rest, seq=seq, has_s0=s0 is not None, emit_state=emit_state)

    res = pl.pallas_call(
        body,
        grid=(n_seq, N_HEAD_GROUPS),
        in_specs=in_specs,
        out_specs=out_specs,
        out_shape=out_shape,
        scratch_shapes=[
            pltpu.VMEM((n_ch, seq, GDN_DK), f32),
            pltpu.VMEM((n_ch, n_chunks, 2 * GDN_CHUNK, GDN_DK), bf16),
            pltpu.VMEM((n_ch, seq, GDN_DK), bf16),
            pltpu.VMEM((n_ch, n_chunks, GDN_CHUNK, GDN_CHUNK), bf16),
            pltpu.VMEM((n_ch, n_chunks, 8, GDN_DK), f32),
            pltpu.VMEM((n_ch, GDN_DK, GDN_DK), f32),
            pltpu.VMEM((2, seq, hw), f32),
        ],
        input_output_aliases=aliases,
        compiler_params=_params(("arbitrary", "arbitrary")),
        name="gdn_core_%d" % seq,
    )(*args)
    return res


def _ctx_attn_kernel(q_ref, k_ref, v_ref, kprev_ref, vprev_ref, o_ref, ko_ref, vo_ref):
    del kprev_ref, vprev_ref
    scale = NA_DH ** -0.5
    k = k_ref[...]
    v = v_ref[...]
    ko_ref[0, 0] = k
    vo_ref[0, 0] = v
    for hp in range(NA_HEADS // 2):
        outs = []
        for hh in range(2):
            hs = slice((2 * hp + hh) * NA_DH, (2 * hp + hh + 1) * NA_DH)
            s = _dot_nt(q_ref[:, hs].astype(bf16), k[:, hs].astype(bf16)) * scale
            e = jnp.exp(s - jnp.max(s, axis=-1, keepdims=True))
            p = e / jnp.sum(e, axis=-1, keepdims=True)
            outs.append(_dot(p.astype(bf16), v[:, hs].astype(bf16)))
        o_ref[:, 2 * hp * NA_DH:(2 * hp + 2) * NA_DH] = jnp.concatenate(outs, axis=1).astype(bf16)


def _ctx_attn(qkv, layer_slot, k_prev, v_prev):
    cache_shape = jax.ShapeDtypeStruct((BATCH, 2, SEQ, D_MODEL), f32)
    in_specs = [pl.BlockSpec((SEQ, D_MODEL), lambda b: (b, 0)),
                pl.BlockSpec((SEQ, D_MODEL), lambda b: (b, 1)),
                pl.BlockSpec((SEQ, D_MODEL), lambda b: (b, 2)),
                pl.BlockSpec(memory_space=pl.ANY),
                pl.BlockSpec(memory_space=pl.ANY)]
    aliases = {}
    if k_prev is None:
        k_prev = jnp.zeros((8, 128), f32)
        v_prev = jnp.zeros((8, 128), f32)
    else:
        aliases = {3: 1, 4: 2}
    cache_spec = pl.BlockSpec((1, 1, SEQ, D_MODEL), lambda b: (b, layer_slot, 0, 0))
    return pl.pallas_call(
        _ctx_attn_kernel,
        grid=(BATCH,),
        in_specs=in_specs,
        out_specs=[pl.BlockSpec((SEQ, D_MODEL), lambda b: (b, 0)), cache_spec, cache_spec],
        out_shape=[jax.ShapeDtypeStruct((N_TOK, D_MODEL), bf16), cache_shape, cache_shape],
        input_output_aliases=aliases,
        compiler_params=_params(("arbitrary",)),
        name="ctx_attn",
    )(qkv, qkv, qkv, k_prev, v_prev)


def _window_row_start(r):
    return min(max(r - NA_KR // 2, 0), GRID_ROWS - NA_KR)


def _na_attn_kernel(q_ref, k_ref, v_ref, ck_ref, cv_ref, bias_ref, og_ref, o_ref):
    del og_ref
    scale = NA_DH ** -0.5
    kk = [k_ref[:, hh * NA_DH:(hh + 1) * NA_DH].astype(bf16) for hh in range(2)]
    vv = [v_ref[:, hh * NA_DH:(hh + 1) * NA_DH].astype(bf16) for hh in range(2)]
    ck = [ck_ref[0, 0, :, hh * NA_DH:(hh + 1) * NA_DH].astype(bf16) for hh in range(2)]
    cv = [cv_ref[0, 0, :, hh * NA_DH:(hh + 1) * NA_DH].astype(bf16) for hh in range(2)]
    for r in range(GRID_ROWS):
        rs = _window_row_start(r)
        d0 = rs - r + NA_KR - 1
        rows = slice(r * GRID_W, (r + 1) * GRID_W)
        win = slice(rs * GRID_W, (rs + NA_KR) * GRID_W)
        outs = []
        for hh in range(2):
            q = q_ref[rows, hh * NA_DH:(hh + 1) * NA_DH].astype(bf16)
            s_loc = _dot_nt(q, kk[hh][win]) * scale + bias_ref[0, hh, d0]
            s_ctx = _dot_nt(q, ck[hh]) * scale
            m = jnp.maximum(jnp.max(s_loc, axis=-1, keepdims=True), jnp.max(s_ctx, axis=-1, keepdims=True))
            e_loc = jnp.exp(s_loc - m)
            e_ctx = jnp.exp(s_ctx - m)
            den = jnp.sum(e_loc, axis=-1, keepdims=True) + jnp.sum(e_ctx, axis=-1, keepdims=True)
            outs.append(_dot((e_loc / den).astype(bf16), vv[hh][win])
                        + _dot((e_ctx / den).astype(bf16), cv[hh]))
        o_ref[rows, :] = jnp.concatenate(outs, axis=1).astype(bf16)


def _na_attn(qkv, cache_k, cache_v, bias, og, layer_slot):
    pw = 2 * NA_DH
    n_pairs = NA_HEADS // 2
    rb0 = N_PROMPT // DEC_SEQ
    return pl.pallas_call(
        _na_attn_kernel,
        grid=(n_pairs, DEC_BATCH),
        in_specs=[
            pl.BlockSpec((DEC_SEQ, pw), lambda hp, b: (rb0 + b, hp)),
            pl.BlockSpec((DEC_SEQ, pw), lambda hp, b: (rb0 + b, n_pairs + hp)),
            pl.BlockSpec((DEC_SEQ, pw), lambda hp, b: (rb0 + b, 2 * n_pairs + hp)),
            pl.BlockSpec((1, 1, PAST_LEN, pw), lambda hp, b: (b, layer_slot, 0, hp)),
            pl.BlockSpec((1, 1, PAST_LEN, pw), lambda hp, b: (b, layer_slot, 0, hp)),
            pl.BlockSpec((1, 2, NA_KR, GRID_W, NA_KR * GRID_W), lambda hp, b: (layer_slot, hp, 0, 0, 0)),
            pl.BlockSpec(memory_space=pl.ANY),
        ],
        out_specs=pl.BlockSpec((DEC_SEQ, pw), lambda hp, b: (rb0 + b, hp)),
        out_shape=jax.ShapeDtypeStruct((N_TOK, D_MODEL), bf16),
        input_output_aliases={6: 0},
        compiler_params=_params(("arbitrary", "arbitrary")),
        name="na_attn",
    )(qkv, qkv, qkv, cache_k, cache_v, bias, og)


def _na_bias_tiles(rel_bias):
    col = np.arange(GRID_W)
    col_start = np.clip(col - NA_KC // 2, 0, GRID_W - NA_KC)
    col_mask = (col[None, :] >= col_start[:, None]) & (col[None, :] < col_start[:, None] + NA_KC)
    col_idx = np.clip(col[None, :] - col[:, None] + NA_KC - 1, 0, 2 * NA_KC - 2)
    g = rel_bias[:, :, :, col_idx]
    g = jnp.where(col_mask[None, None, None], g, NEG_INF)
    tiles = [jnp.transpose(g[:, :, d0:d0 + NA_KR], (0, 1, 3, 2, 4)).reshape(
        g.shape[0], NA_HEADS, GRID_W, NA_KR * GRID_W) for d0 in range(NA_KR)]
    return jnp.stack(tiles, axis=2)


def _post_kernel(x_ref, xp_ref, xn_ref, o_ref, op_ref, on_ref, mod_ref, g_ref, wout_ref, wup_ref,
                 cw_ref, wdn_ref, fg_ref, out_ref, xe_s, oe_s, he_s, acc_s, *, final):
    i = pl.program_id(0)
    ext = TP + 2 * HALO
    xe_s[0:HALO] = xp_ref[...]
    xe_s[HALO:HALO + TP] = x_ref[...]
    xe_s[HALO + TP:ext] = xn_ref[...]
    oe_s[0:HALO] = op_ref[...]
    oe_s[HALO:HALO + TP] = o_ref[...]
    oe_s[HALO + TP:ext] = on_ref[...]
    m = mod_ref[0, 0]
    x1 = xe_s[...] + m[2:3] * _dot(oe_s[...], wout_ref[...])
    xe_s[...] = x1
    he_s[...] = _norm_mod(x1, g_ref[0], m[3:4], m[4:5]).astype(bf16)

    seq = _seq_len_of_tile(i, TP)
    pos = (lax.broadcasted_iota(jnp.int32, (TP, 1), 0) + i * TP) & (seq - 1)
    first = pos == 0
    last = pos == seq - 1
    acc_s[...] = jnp.zeros_like(acc_s)

    def chunk(j, carry):
        up = _dot(he_s[...], wup_ref[j])
        cw = cw_ref[j]
        prev = jnp.where(first, 0.0, pltpu.roll(up, 1, axis=0)[HALO:HALO + TP])
        nxt = jnp.where(last, 0.0, pltpu.roll(up, ext - 1, axis=0)[HALO:HALO + TP])
        u = cw[0:1] * prev + cw[1:2] * up[HALO:HALO + TP] + cw[2:3] * nxt + cw[3:4]
        act = (_silu(u[:, FF_CHUNK:]) * u[:, :FF_CHUNK]).astype(bf16)
        acc_s[...] += _dot(act, wdn_ref[j])
        return carry

    lax.fori_loop(0, N_FF_CHUNKS, chunk, 0)
    y = xe_s[HALO:HALO + TP] + m[5:6] * acc_s[...]
    if final:
        y = _rms(y) * fg_ref[...]
    out_ref[...] = y


def _post(x, og, mod, norm_g, w_out, w_up, conv_wb, w_down, final_g, layer, final):
    n_tiles = N_TOK // TP
    hb = TP // HALO
    n_hb = N_TOK // HALO

    def prev_map(i):
        return (jnp.maximum(i * hb - 1, 0), 0)

    def next_map(i):
        return (jnp.minimum((i + 1) * hb, n_hb - 1), 0)

    def mod_map(i):
        tiles_per_req = DEC_SEQ // TP
        return (layer, jnp.maximum((i - (N_PROMPT // TP - tiles_per_req)) // tiles_per_req, 0), 0, 0)

    resident = dict(pipeline_mode=pl.Buffered(1))
    ext = TP + 2 * HALO
    return pl.pallas_call(
        functools.partial(_post_kernel, final=final),
        grid=(n_tiles,),
        in_specs=[
            pl.BlockSpec((TP, D_MODEL), lambda i: (i, 0)),
            pl.BlockSpec((HALO, D_MODEL), prev_map),
            pl.BlockSpec((HALO, D_MODEL), next_map),
            pl.BlockSpec((TP, D_MODEL), lambda i: (i, 0)),
            pl.BlockSpec((HALO, D_MODEL), prev_map),
            pl.BlockSpec((HALO, D_MODEL), next_map),
            pl.BlockSpec((1, 1, N_MOD, D_MODEL), mod_map),
            pl.BlockSpec((1, 1, D_MODEL), lambda i: (layer, 0, 0)),
            pl.BlockSpec((D_MODEL, D_MODEL), lambda i: (0, 0), **resident),
            pl.BlockSpec((N_FF_CHUNKS, D_MODEL, 2 * FF_CHUNK), lambda i: (0, 0, 0), **resident),
            pl.BlockSpec((N_FF_CHUNKS, 8, 2 * FF_CHUNK), lambda i: (0, 0, 0), **resident),
            pl.BlockSpec((N_FF_CHUNKS, FF_CHUNK, D_MODEL), lambda i: (0, 0, 0), **resident),
            pl.BlockSpec((1, D_MODEL), lambda i: (0, 0)),
        ],
        out_specs=pl.BlockSpec((TP, D_MODEL), lambda i: (i, 0)),
        out_shape=jax.ShapeDtypeStruct((N_TOK, D_MODEL), f32),
        scratch_shapes=[
            pltpu.VMEM((ext, D_MODEL), f32),
            pltpu.VMEM((ext, D_MODEL), bf16),
            pltpu.VMEM((ext, D_MODEL), bf16),
            pltpu.VMEM((TP, D_MODEL), f32),
        ],
        compiler_params=_params(("arbitrary",)),
        name="post",
    )(x, x, x, og, og, og, mod, norm_g, w_out, w_up, conv_wb, w_down, final_g)


def _ffn_weights(w_up, conv_w, conv_b, w_down):
    def pair(t):
        val = t[..., :D_FF].reshape(t.shape[:-1] + (N_FF_CHUNKS, FF_CHUNK))
        gate = t[..., D_FF:].reshape(t.shape[:-1] + (N_FF_CHUNKS, FF_CHUNK))
        return jnp.moveaxis(jnp.concatenate([val, gate], axis=-1), -2, 0)

    up = pair(w_up).astype(bf16)
    taps = pair(jnp.concatenate([conv_w, conv_b[None], jnp.zeros((4, 2 * D_FF), f32)], axis=0))
    down = w_down.reshape(N_FF_CHUNKS, FF_CHUNK, D_MODEL).astype(bf16)
    return up, taps, down


def _gdn_gate_layout(w_in, a_log, dt_bias):
    base = 4 * D_MODEL
    gw = N_HEAD_GROUPS * GATE_LANES
    src = np.zeros((gw,), np.int32)
    used = np.zeros((gw,), bool)
    gate_src = np.zeros((gw,), np.int32)
    is_a = np.zeros((gw,), bool)
    for h in range(GDN_HEADS):
        hg, hl = divmod(h, HEAD_GROUP)
        for kind in range(4):
            lane = hg * GATE_LANES + 8 * hl + kind
            d = kind % 2
            src[lane] = base + (kind // 2) * 2 * GDN_HEADS + d * GDN_HEADS + h
            used[lane] = True
            gate_src[lane] = d * GDN_HEADS + h
            is_a[lane] = kind < 2
    w_ab = jnp.where(used[None, :], w_in[:, src], 0.0).astype(bf16)
    alog_row = jnp.where(is_a, a_log.reshape(-1)[gate_src], 0.0)[None, :]
    dtb_row = jnp.where(is_a, dt_bias.reshape(-1)[gate_src], 0.0)[None, :]
    return w_ab, alog_row.astype(f32), dtb_row.astype(f32)


def kernel(x_prompt, x_sample, state_gdn, cache_k, cache_v, c, c_ctx, w_ada, b_ada, norm1_g, norm2_g,
           gdn_w_in, gdn_conv_w, gdn_a_log, gdn_dt_bias, gdn_norm_g, gdn_w_out,
           na_w_qkv, na_rel_bias, na_w_out, ffn_w_up, ffn_conv_w, ffn_conv_b, ffn_w_down, final_g):
    x = jnp.concatenate([x_prompt.reshape(N_PROMPT, D_MODEL), x_sample.reshape(N_SAMPLE, D_MODEL)], axis=0)
    cvec = jnp.concatenate([c_ctx[None, :], c, jnp.zeros((MOD_ROWS - 1 - DEC_BATCH, D_MODEL), f32)], axis=0)
    mod = _modulation(cvec, w_ada, b_ada).reshape(DEPTH, MOD_ROWS, N_MOD, D_MODEL)
    n1 = norm1_g.reshape(DEPTH, 1, D_MODEL)
    n2 = norm2_g.reshape(DEPTH, 1, D_MODEL)
    fg = final_g.reshape(1, D_MODEL)
    ck = cache_k.reshape(DEC_BATCH, 2, PAST_LEN, D_MODEL)
    cv = cache_v.reshape(DEC_BATCH, 2, PAST_LEN, D_MODEL)
    bias = _na_bias_tiles(na_rel_bias)

    state_out = None
    new_k = None
    new_v = None
    for l in range(DEPTH):
        j = l // 2
        if l % 2 == 0:
            w_in = gdn_w_in[j]
            w_ab, alog_row, dtb_row = _gdn_gate_layout(w_in, gdn_a_log[j], gdn_dt_bias[j])
            qkvz, gcol, grow = _gdn_in(x, mod, n1, w_in[:, :4 * D_MODEL].astype(bf16), w_ab,
                                       gdn_conv_w[j], alog_row, dtb_row, l)
            ng = gdn_norm_g[j].reshape(1, GDN_DK)
            og, state_out = _gdn_core(qkvz, gcol, grow, ng, seq=SEQ, n_seq=BATCH, row0=0, og_prev=None,
                                      state_prev=state_out, state_layer=j, emit_state=True)
            og, = _gdn_core(qkvz, gcol, grow, ng, seq=DEC_SEQ, n_seq=DEC_BATCH, row0=N_PROMPT, og_prev=og,
                            s0=state_gdn, s0_layer=j)
            w_out = gdn_w_out[j]
        else:
            qkv = _na_in(x, mod, n1, na_w_qkv[j].astype(bf16), l)
            og, new_k, new_v = _ctx_attn(qkv, j, new_k, new_v)
            og = _na_attn(qkv, ck, cv, bias, og, j)
            w_out = na_w_out[j]
        up, taps, down = _ffn_weights(ffn_w_up[l], ffn_conv_w[l], ffn_conv_b[l], ffn_w_down[l])
        x = _post(x, og, mod, n2, w_out.astype(bf16), up, taps, down, fg, l, l == DEPTH - 1)

    y_prompt = x[:N_PROMPT].reshape(BATCH, SEQ, D_MODEL)
    y_sample = x[N_PROMPT:].reshape(DEC_BATCH, DEC_SEQ, D_MODEL)
    new_cache_k = new_k.reshape(BATCH, 2, SEQ, NA_HEADS, NA_DH)
    new_cache_v = new_v.reshape(BATCH, 2, SEQ, NA_HEADS, NA_DH)
    return (y_prompt, y_sample, state_out, new_cache_k, new_cache_v)
```

```python
import functools

import jax
import jax.numpy as jnp
import numpy as np
from jax import lax
from jax.experimental import pallas as pl
from jax.experimental.pallas import tpu as pltpu

f32 = jnp.float32
bf16 = jnp.bfloat16

D_MODEL = 1024
BATCH = 16
SEQ = 256
DEPTH = 4
DEC_BATCH = 8
DEC_SEQ = 1024
PAST_LEN = 256
GRID_W = 64
GRID_ROWS = DEC_SEQ // GRID_W
GDN_HEADS = 8
GDN_DK = 128
GDN_CHUNK = 64
NA_HEADS = 16
NA_DH = 64
NA_KR = 8
NA_KC = 16
D_FF = 2816
N_MOD = 6
EPS = 1e-6
NEG_INF = -1e30

N_PROMPT = BATCH * SEQ
N_SAMPLE = DEC_BATCH * DEC_SEQ
N_TOK = N_PROMPT + N_SAMPLE

TM = 1024
TP = 512
HALO = 16
FF_CHUNK = 256
N_FF_CHUNKS = D_FF // FF_CHUNK
HEAD_GROUP = 4
N_HEAD_GROUPS = GDN_HEADS // HEAD_GROUP
GATE_LANES = 128
PREP_CHUNKS = 2
NA_ROW_GROUP = 4
MOD_ROWS = 16
MOD_TN = 1536

VMEM_LIMIT = 56 * 1024 * 1024


def _silu(x):
    return x * jax.nn.sigmoid(x)


def _dot(a, b):
    return jnp.dot(a, b, preferred_element_type=f32)


def _dot_nt(a, b):
    return lax.dot_general(a, b, (((1,), (1,)), ((), ())), preferred_element_type=f32)


def _dot_tn(a, b):
    return lax.dot_general(a, b, (((0,), (0,)), ((), ())), preferred_element_type=f32)


def _rms(x):
    return x * lax.rsqrt(jnp.mean(x * x, axis=-1, keepdims=True) + EPS)


def _params(sem, vmem=VMEM_LIMIT):
    return pltpu.CompilerParams(dimension_semantics=sem, vmem_limit_bytes=vmem)


def _mod_kernel(cv_ref, w_ref, b_ref, o_ref):
    s = _silu(cv_ref[...]).astype(bf16)
    o_ref[0] = _dot(s, w_ref[0].astype(bf16)) + b_ref[0]


def _modulation(cvec, w_ada, b_ada):
    n_tiles = (N_MOD * D_MODEL) // MOD_TN
    return pl.pallas_call(
        _mod_kernel,
        grid=(DEPTH, n_tiles),
        in_specs=[
            pl.BlockSpec((MOD_ROWS, D_MODEL), lambda l, n: (0, 0)),
            pl.BlockSpec((1, D_MODEL, MOD_TN), lambda l, n: (l, 0, n)),
            pl.BlockSpec((1, 1, MOD_TN), lambda l, n: (l, 0, n)),
        ],
        out_specs=pl.BlockSpec((1, MOD_ROWS, MOD_TN), lambda l, n: (l, 0, n)),
        out_shape=jax.ShapeDtypeStruct((DEPTH, MOD_ROWS, N_MOD * D_MODEL), f32),
        compiler_params=_params(("arbitrary", "arbitrary")),
        name="modulation",
    )(cvec, w_ada, b_ada.reshape(DEPTH, 1, N_MOD * D_MODEL))


def _mod_row_tm(i):
    return jnp.maximum(i - (N_PROMPT // TM - 1), 0)


def _norm_mod(x, g, shift, scale):
    return (_rms(x) * g) * (1.0 + scale) + shift


def _na_in_kernel(x_ref, mod_ref, g_ref, w_ref, o_ref, h_ref):
    @pl.when(pl.program_id(1) == 0)
    def _():
        m = mod_ref[0, 0]
        h_ref[...] = _norm_mod(x_ref[...], g_ref[0], m[0:1], m[1:2]).astype(bf16)

    o_ref[...] = _dot(h_ref[...], w_ref[...])


def _na_in(x, mod, norm_g, w, layer):
    n_out = w.shape[1] // D_MODEL
    return pl.pallas_call(
        _na_in_kernel,
        grid=(N_TOK // TM, n_out),
        in_specs=[
            pl.BlockSpec((TM, D_MODEL), lambda i, n: (i, 0)),
            pl.BlockSpec((1, 1, N_MOD, D_MODEL), lambda i, n: (layer, _mod_row_tm(i), 0, 0)),
            pl.BlockSpec((1, 1, D_MODEL), lambda i, n: (layer, 0, 0)),
            pl.BlockSpec((D_MODEL, D_MODEL), lambda i, n: (0, n)),
        ],
        out_specs=pl.BlockSpec((TM, D_MODEL), lambda i, n: (i, n)),
        out_shape=jax.ShapeDtypeStruct((N_TOK, w.shape[1]), f32),
        scratch_shapes=[pltpu.VMEM((TM, D_MODEL), bf16)],
        compiler_params=_params(("arbitrary", "arbitrary")),
        name="na_in",
    )(x, mod, norm_g, w)


def _seq_len_of_tile(i, tile):
    return jnp.where(i < N_PROMPT // tile, SEQ, DEC_SEQ)


def _gdn_in_kernel(x_ref, mod_ref, g_ref, w_ref, wab_ref, cw_ref, alog_ref, dtb_ref,
                   o_ref, gcol_ref, grow_ref, h_ref):
    i = pl.program_id(0)
    n = pl.program_id(1)

    @pl.when(n == 0)
    def _():
        m = mod_ref[0, 0]
        h_ref[...] = _norm_mod(x_ref[...], g_ref[0], m[0:1], m[1:2]).astype(bf16)

    def conv_silu():
        p = _dot(h_ref[...], w_ref[...])
        seq = _seq_len_of_tile(i, TM)
        pos = lax.broadcasted_iota(jnp.int32, (TM, 1), 0) & (seq - 1)
        prev = jnp.where(pos == 0, 0.0, pltpu.roll(p, 1, axis=0))
        nxt = jnp.where(pos == seq - 1, 0.0, pltpu.roll(p, TM - 1, axis=0))
        cw = cw_ref[...]
        return _silu(cw[0:1] * prev + cw[1:2] * p + cw[2:3] * nxt)

    @pl.when(n < 2)
    def _():
        s = conv_silu()
        qscale = jnp.where(n == 0, GDN_DK ** -0.5, 1.0).astype(f32)
        for h in range(GDN_HEADS):
            sh = s[:, h * GDN_DK:(h + 1) * GDN_DK]
            nh = sh * lax.rsqrt(jnp.sum(sh * sh, axis=-1, keepdims=True) + EPS)
            o_ref[:, h * GDN_DK:(h + 1) * GDN_DK] = nh * qscale

    @pl.when(n == 2)
    def _():
        o_ref[...] = conv_silu()

    @pl.when(n == 3)
    def _():
        o_ref[...] = _dot(h_ref[...], w_ref[...])

    @pl.when(n == 4)
    def _():
        ab = _dot(h_ref[...], wab_ref[...])
        width = ab.shape[1]
        kind = lax.broadcasted_iota(jnp.int32, (1, width), 1) & 7
        xs = ab + dtb_ref[...]
        softplus = jnp.maximum(xs, 0.0) + jnp.log1p(jnp.exp(-jnp.abs(xs)))
        g = -jnp.exp(alog_ref[...]) * softplus
        pos = lax.broadcasted_iota(jnp.int32, (TM, 1), 0) & (GDN_CHUNK - 1)
        cf = g
        cb = g
        s = 1
        while s < GDN_CHUNK:
            cf = cf + jnp.where(pos >= s, pltpu.roll(cf, s, axis=0), 0.0)
            cb = cb + jnp.where(pos < GDN_CHUNK - s, pltpu.roll(cb, TM - s, axis=0), 0.0)
            s *= 2
        gates = jnp.where(kind == 0, cf, jnp.where(kind == 1, cb, jnp.where(kind < 4, jax.nn.sigmoid(ab), 0.0)))
        gcol_ref[...] = gates
        for c in range(TM // GDN_CHUNK):
            for hg in range(width // GATE_LANES):
                blk = gates[c * GDN_CHUNK:(c + 1) * GDN_CHUNK, hg * GATE_LANES:(hg + 1) * GATE_LANES]
                grow_ref[c, hg * GATE_LANES:(hg + 1) * GATE_LANES, :] = blk.T


def _gdn_in(x, mod, norm_g, w_qkvz, w_ab, conv_w, alog_row, dtb_row, layer):
    n_chunks = N_TOK // GDN_CHUNK
    gw = N_HEAD_GROUPS * GATE_LANES
    n_proj = w_qkvz.shape[1] // D_MODEL
    return pl.pallas_call(
        _gdn_in_kernel,
        grid=(N_TOK // TM, n_proj + 1),
        in_specs=[
            pl.BlockSpec((TM, D_MODEL), lambda i, n: (i, 0)),
            pl.BlockSpec((1, 1, N_MOD, D_MODEL), lambda i, n: (layer, _mod_row_tm(i), 0, 0)),
            pl.BlockSpec((1, 1, D_MODEL), lambda i, n: (layer, 0, 0)),
            pl.BlockSpec((D_MODEL, D_MODEL), lambda i, n: (0, jnp.minimum(n, n_proj - 1))),
            pl.BlockSpec((D_MODEL, gw), lambda i, n: (0, 0)),
            pl.BlockSpec((3, D_MODEL), lambda i, n: (0, jnp.minimum(n, 2))),
            pl.BlockSpec((1, gw), lambda i, n: (0, 0)),
            pl.BlockSpec((1, gw), lambda i, n: (0, 0)),
        ],
        out_specs=[
            pl.BlockSpec((TM, D_MODEL), lambda i, n: (i, jnp.minimum(n, n_proj - 1))),
            pl.BlockSpec((TM, gw), lambda i, n: (i, 0)),
            pl.BlockSpec((TM // GDN_CHUNK, gw, GDN_CHUNK), lambda i, n: (i, 0, 0)),
        ],
        out_shape=[
            jax.ShapeDtypeStruct((N_TOK, w_qkvz.shape[1]), f32),
            jax.ShapeDtypeStruct((N_TOK, gw), f32),
            jax.ShapeDtypeStruct((n_chunks, gw, GDN_CHUNK), f32),
        ],
        scratch_shapes=[pltpu.VMEM((TM, D_MODEL), bf16)],
        compiler_params=_params(("arbitrary", "arbitrary")),
        name="gdn_in",
    )(x, mod, norm_g, w_qkvz, w_ab, conv_w, alog_row, dtb_row)


def _tri_inverse_many(a_list, eye, level_masks):
    ts = [eye - jnp.where(level_masks[0], a, 0.0) for a in a_list]
    for mask in level_masks[1:]:
        t16s = [t.astype(bf16) for t in ts]
        tls = [_dot(t16, jnp.where(mask, a, 0.0).astype(bf16)) for t16, a in zip(t16s, a_list)]
        ts = [t - _dot(tl.astype(bf16), t16) for t, tl, t16 in zip(ts, tls, t16s)]
    return ts


def _gdn_core_kernel(*refs, seq, has_s0, emit_state):
    q_ref, k_ref, v_ref, z_ref, gcol_ref, grow_ref, ng_ref = refs[:7]
    pos = 7
    s0_ref = None
    if has_s0:
        s0_ref = refs[pos]
        pos += 1
    pos += 1
    o_ref = refs[pos]
    pos += 1
    sfin_ref = None
    if emit_state:
        sfin_ref = refs[pos]
        pos += 1
    u_s, wq_s, kt_s, in_s, eg_s, st_s, oo_s = refs[pos:]

    C = GDN_CHUNK
    n_seq = TM // seq
    chunks_per_seq = seq // C
    ii = lax.broadcasted_iota(jnp.int32, (C, C), 0)
    jj = lax.broadcasted_iota(jnp.int32, (C, C), 1)
    eye = (ii == jj).astype(f32)
    incl = (ii >= jj, ii <= jj)
    strict = (ii > jj, ii < jj)
    level_masks = []
    sh = 0
    while (1 << sh) < C:
        level_masks.append(((ii >> (sh + 1)) == (jj >> (sh + 1))) & ((ii >> sh) != (jj >> sh)))
        sh += 1

    def prep(it, carry):
        units = []
        for cc in range(PREP_CHUNKS):
            c = it * PREP_CHUNKS + cc
            rows = pl.ds(pl.multiple_of(c * C, C), C)
            gc_tile = gcol_ref[rows, :]
            gr_tile = grow_ref[c]
            for hl in range(HEAD_GROUP):
                ls = slice(hl * GDN_DK, (hl + 1) * GDN_DK)
                kc = k_ref[rows, ls]
                units.append(dict(c=c, rows=rows, hl=hl, kc=kc, qc=q_ref[rows, ls], vc=v_ref[rows, ls],
                                  k16=kc.astype(bf16), gc=gc_tile, gr=gr_tile))
        for un in units:
            un["gram"] = _dot_nt(un["k16"], un["k16"])
            un["qk"] = _dot_nt(un["qc"].astype(bf16), un["k16"])
        chains = []
        for un in units:
            for d in range(2):
                lane = 8 * un["hl"] + d
                gcol = un["gc"][:, lane:lane + 1]
                bcol = un["gc"][:, lane + 2:lane + 3]
                grow = un["gr"][lane:lane + 1, :]
                diff = gcol - grow
                decay = jnp.where(incl[d], jnp.exp(jnp.where(incl[d], diff, 0.0)), 0.0)
                a = jnp.where(strict[d], un["gram"] * decay * bcol, 0.0)
                chains.append(dict(un=un, d=d, gcol=gcol, bcol=bcol, decay=decay, a=a))
        ts = _tri_inverse_many([ch["a"] for ch in chains], eye, level_masks)
        uws = []
        for ch, t in zip(chains, ts):
            un = ch["un"]
            egc = jnp.exp(ch["gcol"])
            rhs = jnp.concatenate([(un["vc"] * ch["bcol"]).astype(bf16),
                                   (un["kc"] * (ch["bcol"] * egc)).astype(bf16)], axis=1)
            ch["egc"] = egc
            uws.append(_dot(t.astype(bf16), rhs))
        for ch, uw in zip(chains, uws):
            un = ch["un"]
            d = ch["d"]
            idx = un["hl"] * 2 + d
            gcol = ch["gcol"]
            glast = gcol[C - 1:C] if d == 0 else gcol[0:1]
            u_s[idx, un["rows"], :] = uw[:, :GDN_DK]
            wq_s[idx, un["c"], 0:C, :] = uw[:, GDN_DK:].astype(bf16)
            wq_s[idx, un["c"], C:2 * C, :] = (un["qc"] * ch["egc"]).astype(bf16)
            kt_s[idx, un["rows"], :] = (un["kc"] * jnp.exp(glast - gcol)).astype(bf16)
            in_s[idx, un["c"]] = (un["qk"] * ch["decay"]).astype(bf16)
            eg_s[idx, un["c"]] = jnp.broadcast_to(jnp.exp(glast), (8, GDN_DK))
        return carry

    lax.fori_loop(0, TM // C // PREP_CHUNKS, prep, 0)

    lanes = []
    for si in range(n_seq):
        for hl in range(HEAD_GROUP):
            for d in range(2):
                lanes.append((si, hl, d))
    for li, (si, hl, d) in enumerate(lanes):
        st_s[li] = s0_ref[si, 0, d, hl] if has_s0 else jnp.zeros((GDN_DK, GDN_DK), f32)

    def scan(t, carry):
        work = []
        for li, (si, hl, d) in enumerate(lanes):
            c = si * chunks_per_seq + (t if d == 0 else chunks_per_seq - 1 - t)
            rows = pl.ds(pl.multiple_of(c * C, C), C)
            work.append(dict(li=li, idx=hl * 2 + d, hl=hl, d=d, c=c, rows=rows, s=st_s[li]))
        for w in work:
            w["s16"] = w["s"].astype(bf16)
            w["wsqs"] = _dot(wq_s[w["idx"], w["c"]], w["s16"])
        for w in work:
            w["v16"] = (u_s[w["idx"], w["rows"], :] - w["wsqs"][:C]).astype(bf16)
            w["o"] = w["wsqs"][C:] + _dot(in_s[w["idx"], w["c"]], w["v16"])
            w["ds"] = _dot_tn(kt_s[w["idx"], w["rows"], :], w["v16"])
        for w in work:
            ls = slice(w["hl"] * GDN_DK, (w["hl"] + 1) * GDN_DK)
            oo_s[w["d"], w["rows"], ls] = w["o"]
            st_s[w["li"]] = w["s"] * eg_s[w["idx"], w["c"]][0:1, :] + w["ds"]
        return carry

    lax.fori_loop(0, chunks_per_seq, scan, 0)

    ng = ng_ref[...]
    for hl in range(HEAD_GROUP):
        ls = slice(hl * GDN_DK, (hl + 1) * GDN_DK)
        o = oo_s[0, :, ls] + oo_s[1, :, ls]
        o_ref[:, ls] = (_rms(o) * ng * _silu(z_ref[:, ls])).astype(bf16)
    if emit_state:
        for li, (si, hl, d) in enumerate(lanes):
            sfin_ref[si, 0, d, hl] = st_s[li]


def _gdn_core(qkvz, gcol, grow, norm_g_row, *, seq, tile0, n_tiles, og_prev, s0=None, s0_layer=0,
              state_prev=None, state_layer=0, emit_state=False):
    hw = HEAD_GROUP * GDN_DK
    n_seq = TM // seq
    n_chunks = TM // GDN_CHUNK
    n_ch = HEAD_GROUP * 2
    state_block = (n_seq, 1, 2, HEAD_GROUP, GDN_DK, GDN_DK)

    def col_spec(base):
        return pl.BlockSpec((TM, hw), lambda s, hg: (tile0 + s, base * N_HEAD_GROUPS + hg))

    in_specs = [col_spec(0), col_spec(1), col_spec(2), col_spec(3),
                pl.BlockSpec((TM, GATE_LANES), lambda s, hg: (tile0 + s, hg)),
                pl.BlockSpec((n_chunks, GATE_LANES, GDN_CHUNK), lambda s, hg: (tile0 + s, hg, 0)),
                pl.BlockSpec((1, GDN_DK), lambda s, hg: (0, 0))]
    args = [qkvz, qkvz, qkvz, qkvz, gcol, grow, norm_g_row]
    if s0 is not None:
        in_specs.append(pl.BlockSpec(state_block, lambda s, hg: (s, s0_layer, 0, hg, 0, 0)))
        args.append(s0)
    aliases = {}
    out_specs = [pl.BlockSpec((TM, hw), lambda s, hg: (tile0 + s, hg))]
    out_shape = [jax.ShapeDtypeStruct((N_TOK, D_MODEL), bf16)]
    in_specs.append(pl.BlockSpec(memory_space=pl.ANY))
    if og_prev is None:
        og_prev = jnp.zeros((8, 128), bf16)
    else:
        aliases[len(args)] = 0
    args.append(og_prev)
    n_kernel_inputs = len(args)
    if emit_state:
        out_specs.append(pl.BlockSpec(state_block, lambda s, hg: (s, state_layer, 0, hg, 0, 0)))
        out_shape.append(jax.ShapeDtypeStruct((BATCH, 2, 2, GDN_HEADS, GDN_DK, GDN_DK), f32))
        if state_prev is not None:
            in_specs.append(pl.BlockSpec(memory_space=pl.ANY))
            aliases[len(args)] = 1
            args.append(state_prev)

    def body(*refs):
        _gdn_core_kernel(*refs[:n_kernel_inputs], *refs[len(args):], seq=seq, has_s0=s0 is not None,
                         emit_state=emit_state)

    return pl.pallas_call(
        body,
        grid=(n_tiles, N_HEAD_GROUPS),
        in_specs=in_specs,
        out_specs=out_specs,
        out_shape=out_shape,
        scratch_shapes=[
            pltpu.VMEM((n_ch, TM, GDN_DK), f32),
            pltpu.VMEM((n_ch, n_chunks, 2 * GDN_CHUNK, GDN_DK), bf16),
            pltpu.VMEM((n_ch, TM, GDN_DK), bf16),
            pltpu.VMEM((n_ch, n_chunks, GDN_CHUNK, GDN_CHUNK), bf16),
            pltpu.VMEM((n_ch, n_chunks, 8, GDN_DK), f32),
            pltpu.VMEM((n_seq * n_ch, GDN_DK, GDN_DK), f32),
            pltpu.VMEM((2, TM, hw), f32),
        ],
        input_output_aliases=aliases,
        compiler_params=_params(("arbitrary", "arbitrary")),
        name="gdn_core_%d" % seq,
    )(*args)


def _ctx_attn_kernel(q_ref, k_ref, v_ref, kprev_ref, vprev_ref, o_ref, ko_ref, vo_ref):
    del kprev_ref, vprev_ref
    scale = NA_DH ** -0.5
    k = k_ref[...]
    v = v_ref[...]
    ko_ref[0, 0] = k
    vo_ref[0, 0] = v
    heads = []
    for h in range(NA_HEADS):
        hs = slice(h * NA_DH, (h + 1) * NA_DH)
        heads.append(dict(s=_dot_nt(q_ref[:, hs].astype(bf16), k[:, hs].astype(bf16)) * scale,
                          v16=v[:, hs].astype(bf16)))
    for hd in heads:
        e = jnp.exp(hd["s"] - jnp.max(hd["s"], axis=-1, keepdims=True))
        hd["p16"] = (e / jnp.sum(e, axis=-1, keepdims=True)).astype(bf16)
    outs = [_dot(hd["p16"], hd["v16"]) for hd in heads]
    for hp in range(NA_HEADS // 2):
        o_ref[:, 2 * hp * NA_DH:(2 * hp + 2) * NA_DH] = jnp.concatenate(
            outs[2 * hp:2 * hp + 2], axis=1).astype(bf16)


def _ctx_attn(qkv, layer_slot, k_prev, v_prev):
    cache_shape = jax.ShapeDtypeStruct((BATCH, 2, SEQ, D_MODEL), f32)
    in_specs = [pl.BlockSpec((SEQ, D_MODEL), lambda b: (b, 0)),
                pl.BlockSpec((SEQ, D_MODEL), lambda b: (b, 1)),
                pl.BlockSpec((SEQ, D_MODEL), lambda b: (b, 2)),
                pl.BlockSpec(memory_space=pl.ANY),
                pl.BlockSpec(memory_space=pl.ANY)]
    aliases = {}
    if k_prev is None:
        k_prev = jnp.zeros((8, 128), f32)
        v_prev = jnp.zeros((8, 128), f32)
    else:
        aliases = {3: 1, 4: 2}
    cache_spec = pl.BlockSpec((1, 1, SEQ, D_MODEL), lambda b: (b, layer_slot, 0, 0))
    return pl.pallas_call(
        _ctx_attn_kernel,
        grid=(BATCH,),
        in_specs=in_specs,
        out_specs=[pl.BlockSpec((SEQ, D_MODEL), lambda b: (b, 0)), cache_spec, cache_spec],
        out_shape=[jax.ShapeDtypeStruct((N_TOK, D_MODEL), bf16), cache_shape, cache_shape],
        input_output_aliases=aliases,
        compiler_params=_params(("arbitrary",)),
        name="ctx_attn",
    )(qkv, qkv, qkv, k_prev, v_prev)


def _window_row_start(r):
    return min(max(r - NA_KR // 2, 0), GRID_ROWS - NA_KR)


def _na_attn_kernel(q_ref, k_ref, v_ref, ck_ref, cv_ref, bias_ref, og_ref, o_ref):
    del og_ref
    scale = NA_DH ** -0.5
    kk = [k_ref[:, hh * NA_DH:(hh + 1) * NA_DH].astype(bf16) for hh in range(2)]
    vv = [v_ref[:, hh * NA_DH:(hh + 1) * NA_DH].astype(bf16) for hh in range(2)]
    ck = [ck_ref[0, 0, :, hh * NA_DH:(hh + 1) * NA_DH].astype(bf16) for hh in range(2)]
    cv = [cv_ref[0, 0, :, hh * NA_DH:(hh + 1) * NA_DH].astype(bf16) for hh in range(2)]
    for r0 in range(0, GRID_ROWS, NA_ROW_GROUP):
        units = []
        for r in range(r0, r0 + NA_ROW_GROUP):
            rs = _window_row_start(r)
            for hh in range(2):
                units.append(dict(r=r, hh=hh, d0=rs - r + NA_KR - 1,
                                  win=slice(rs * GRID_W, (rs + NA_KR) * GRID_W),
                                  q=q_ref[r * GRID_W:(r + 1) * GRID_W, hh * NA_DH:(hh + 1) * NA_DH].astype(bf16)))
        for un in units:
            hh = un["hh"]
            un["s_loc"] = _dot_nt(un["q"], kk[hh][un["win"]]) * scale + bias_ref[0, hh, un["d0"]]
            un["s_ctx"] = _dot_nt(un["q"], ck[hh]) * scale
        for un in units:
            m = jnp.maximum(jnp.max(un["s_loc"], axis=-1, keepdims=True),
                            jnp.max(un["s_ctx"], axis=-1, keepdims=True))
            e_loc = jnp.exp(un["s_loc"] - m)
            e_ctx = jnp.exp(un["s_ctx"] - m)
            den = jnp.sum(e_loc, axis=-1, keepdims=True) + jnp.sum(e_ctx, axis=-1, keepdims=True)
            un["p_loc"] = (e_loc / den).astype(bf16)
            un["p_ctx"] = (e_ctx / den).astype(bf16)
        for un in units:
            hh = un["hh"]
            un["o"] = _dot(un["p_loc"], vv[hh][un["win"]]) + _dot(un["p_ctx"], cv[hh])
        for k in range(NA_ROW_GROUP):
            r = r0 + k
            o_ref[r * GRID_W:(r + 1) * GRID_W, :] = jnp.concatenate(
                [units[2 * k]["o"], units[2 * k + 1]["o"]], axis=1).astype(bf16)


def _na_attn(qkv, cache_k, cache_v, bias, og, layer_slot):
    pw = 2 * NA_DH
    n_pairs = NA_HEADS // 2
    rb0 = N_PROMPT // DEC_SEQ
    return pl.pallas_call(
        _na_attn_kernel,
        grid=(n_pairs, DEC_BATCH),
        in_specs=[
            pl.BlockSpec((DEC_SEQ, pw), lambda hp, b: (rb0 + b, hp)),
            pl.BlockSpec((DEC_SEQ, pw), lambda hp, b: (rb0 + b, n_pairs + hp)),
            pl.BlockSpec((DEC_SEQ, pw), lambda hp, b: (rb0 + b, 2 * n_pairs + hp)),
            pl.BlockSpec((1, 1, PAST_LEN, pw), lambda hp, b: (b, layer_slot, 0, hp)),
            pl.BlockSpec((1, 1, PAST_LEN, pw), lambda hp, b: (b, layer_slot, 0, hp)),
            pl.BlockSpec((1, 2, NA_KR, GRID_W, NA_KR * GRID_W), lambda hp, b: (layer_slot, hp, 0, 0, 0)),
            pl.BlockSpec(memory_space=pl.ANY),
        ],
        out_specs=pl.BlockSpec((DEC_SEQ, pw), lambda hp, b: (rb0 + b, hp)),
        out_shape=jax.ShapeDtypeStruct((N_TOK, D_MODEL), bf16),
        input_output_aliases={6: 0},
        compiler_params=_params(("arbitrary", "arbitrary")),
        name="na_attn",
    )(qkv, qkv, qkv, cache_k, cache_v, bias, og)


def _na_bias_tiles(rel_bias):
    col = np.arange(GRID_W)
    col_start = np.clip(col - NA_KC // 2, 0, GRID_W - NA_KC)
    col_mask = (col[None, :] >= col_start[:, None]) & (col[None, :] < col_start[:, None] + NA_KC)
    col_idx = np.clip(col[None, :] - col[:, None] + NA_KC - 1, 0, 2 * NA_KC - 2)
    g = rel_bias[:, :, :, col_idx]
    g = jnp.where(col_mask[None, None, None], g, NEG_INF)
    tiles = [jnp.transpose(g[:, :, d0:d0 + NA_KR], (0, 1, 3, 2, 4)).reshape(
        g.shape[0], NA_HEADS, GRID_W, NA_KR * GRID_W) for d0 in range(NA_KR)]
    return jnp.stack(tiles, axis=2)


def _post_kernel(x_ref, xp_ref, xn_ref, o_ref, op_ref, on_ref, mod_ref, g_ref, wout_ref, wup_ref,
                 cw_ref, wdn_ref, fg_ref, out_ref, xe_s, oe_s, he_s, acc_s, *, final):
    i = pl.program_id(0)
    ext = TP + 2 * HALO
    xe_s[0:HALO] = xp_ref[...]
    xe_s[HALO:HALO + TP] = x_ref[...]
    xe_s[HALO + TP:ext] = xn_ref[...]
    oe_s[0:HALO] = op_ref[...]
    oe_s[HALO:HALO + TP] = o_ref[...]
    oe_s[HALO + TP:ext] = on_ref[...]
    m = mod_ref[0, 0]
    x1 = xe_s[...] + m[2:3] * _dot(oe_s[...], wout_ref[...])
    xe_s[...] = x1
    he_s[...] = _norm_mod(x1, g_ref[0], m[3:4], m[4:5]).astype(bf16)

    seq = _seq_len_of_tile(i, TP)
    pos = (lax.broadcasted_iota(jnp.int32, (TP, 1), 0) + i * TP) & (seq - 1)
    first = pos == 0
    last = pos == seq - 1

    for j in range(N_FF_CHUNKS):
        up = _dot(he_s[...], wup_ref[j])
        cw = cw_ref[j]
        prev = jnp.where(first, 0.0, pltpu.roll(up, 1, axis=0)[HALO:HALO + TP])
        nxt = jnp.where(last, 0.0, pltpu.roll(up, ext - 1, axis=0)[HALO:HALO + TP])
        u = cw[0:1] * prev + cw[1:2] * up[HALO:HALO + TP] + cw[2:3] * nxt + cw[3:4]
        act = (_silu(u[:, FF_CHUNK:]) * u[:, :FF_CHUNK]).astype(bf16)
        down = _dot(act, wdn_ref[j])
        if j == 0:
            acc_s[...] = down
        else:
            acc_s[...] += down

    y = xe_s[HALO:HALO + TP] + m[5:6] * acc_s[...]
    if final:
        y = _rms(y) * fg_ref[...]
    out_ref[...] = y


def _post(x, og, mod, norm_g, w_out, w_up, conv_wb, w_down, final_g, layer, final):
    n_tiles = N_TOK // TP
    hb = TP // HALO
    n_hb = N_TOK // HALO

    def prev_map(i):
        return (jnp.maximum(i * hb - 1, 0), 0)

    def next_map(i):
        return (jnp.minimum((i + 1) * hb, n_hb - 1), 0)

    def mod_map(i):
        tiles_per_req = DEC_SEQ // TP
        return (layer, jnp.maximum((i - (N_PROMPT // TP - tiles_per_req)) // tiles_per_req, 0), 0, 0)

    resident = dict(pipeline_mode=pl.Buffered(1))
    ext = TP + 2 * HALO
    return pl.pallas_call(
        functools.partial(_post_kernel, final=final),
        grid=(n_tiles,),
        in_specs=[
            pl.BlockSpec((TP, D_MODEL), lambda i: (i, 0)),
            pl.BlockSpec((HALO, D_MODEL), prev_map),
            pl.BlockSpec((HALO, D_MODEL), next_map),
            pl.BlockSpec((TP, D_MODEL), lambda i: (i, 0)),
            pl.BlockSpec((HALO, D_MODEL), prev_map),
            pl.BlockSpec((HALO, D_MODEL), next_map),
            pl.BlockSpec((1, 1, N_MOD, D_MODEL), mod_map),
            pl.BlockSpec((1, 1, D_MODEL), lambda i: (layer, 0, 0)),
            pl.BlockSpec((D_MODEL, D_MODEL), lambda i: (0, 0), **resident),
            pl.BlockSpec((N_FF_CHUNKS, D_MODEL, 2 * FF_CHUNK), lambda i: (0, 0, 0), **resident),
            pl.BlockSpec((N_FF_CHUNKS, 8, 2 * FF_CHUNK), lambda i: (0, 0, 0), **resident),
            pl.BlockSpec((N_FF_CHUNKS, FF_CHUNK, D_MODEL), lambda i: (0, 0, 0), **resident),
            pl.BlockSpec((1, D_MODEL), lambda i: (0, 0)),
        ],
        out_specs=pl.BlockSpec((TP, D_MODEL), lambda i: (i, 0)),
        out_shape=jax.ShapeDtypeStruct((N_TOK, D_MODEL), f32),
        scratch_shapes=[
            pltpu.VMEM((ext, D_MODEL), f32),
            pltpu.VMEM((ext, D_MODEL), bf16),
            pltpu.VMEM((ext, D_MODEL), bf16),
            pltpu.VMEM((TP, D_MODEL), f32),
        ],
        compiler_params=_params(("arbitrary",)),
        name="post",
    )(x, x, x, og, og, og, mod, norm_g, w_out, w_up, conv_wb, w_down, final_g)


def _ffn_weights(w_up, conv_w, conv_b, w_down):
    def pair(t):
        val = t[..., :D_FF].reshape(t.shape[:-1] + (N_FF_CHUNKS, FF_CHUNK))
        gate = t[..., D_FF:].reshape(t.shape[:-1] + (N_FF_CHUNKS, FF_CHUNK))
        return jnp.moveaxis(jnp.concatenate([val, gate], axis=-1), -2, 0)

    up = pair(w_up).astype(bf16)
    taps = pair(jnp.concatenate([conv_w, conv_b[None], jnp.zeros((4, 2 * D_FF), f32)], axis=0))
    down = w_down.reshape(N_FF_CHUNKS, FF_CHUNK, D_MODEL).astype(bf16)
    return up, taps, down


def _gdn_gate_layout(w_in, a_log, dt_bias):
    base = 4 * D_MODEL
    gw = N_HEAD_GROUPS * GATE_LANES
    src = np.zeros((gw,), np.int32)
    used = np.zeros((gw,), bool)
    gate_src = np.zeros((gw,), np.int32)
    is_a = np.zeros((gw,), bool)
    for h in range(GDN_HEADS):
        hg, hl = divmod(h, HEAD_GROUP)
        for kind in range(4):
            lane = hg * GATE_LANES + 8 * hl + kind
            d = kind % 2
            src[lane] = base + (kind // 2) * 2 * GDN_HEADS + d * GDN_HEADS + h
            used[lane] = True
            gate_src[lane] = d * GDN_HEADS + h
            is_a[lane] = kind < 2
    w_ab = jnp.where(used[None, :], w_in[:, src], 0.0).astype(bf16)
    alog_row = jnp.where(is_a, a_log.reshape(-1)[gate_src], 0.0)[None, :]
    dtb_row = jnp.where(is_a, dt_bias.reshape(-1)[gate_src], 0.0)[None, :]
    return w_ab, alog_row.astype(f32), dtb_row.astype(f32)


def kernel(x_prompt, x_sample, state_gdn, cache_k, cache_v, c, c_ctx, w_ada, b_ada, norm1_g, norm2_g,
           gdn_w_in, gdn_conv_w, gdn_a_log, gdn_dt_bias, gdn_norm_g, gdn_w_out,
           na_w_qkv, na_rel_bias, na_w_out, ffn_w_up, ffn_conv_w, ffn_conv_b, ffn_w_down, final_g):
    x = jnp.concatenate([x_prompt.reshape(N_PROMPT, D_MODEL), x_sample.reshape(N_SAMPLE, D_MODEL)], axis=0)
    cvec = jnp.concatenate([c_ctx[None, :], c, jnp.zeros((MOD_ROWS - 1 - DEC_BATCH, D_MODEL), f32)], axis=0)
    mod = _modulation(cvec, w_ada, b_ada).reshape(DEPTH, MOD_ROWS, N_MOD, D_MODEL)
    n1 = norm1_g.reshape(DEPTH, 1, D_MODEL)
    n2 = norm2_g.reshape(DEPTH, 1, D_MODEL)
    fg = final_g.reshape(1, D_MODEL)
    ck = cache_k.reshape(DEC_BATCH, 2, PAST_LEN, D_MODEL)
    cv = cache_v.reshape(DEC_BATCH, 2, PAST_LEN, D_MODEL)
    bias = _na_bias_tiles(na_rel_bias)
    prompt_tiles = N_PROMPT // TM

    state_out = None
    new_k = None
    new_v = None
    for l in range(DEPTH):
        j = l // 2
        if l % 2 == 0:
            w_in = gdn_w_in[j]
            w_ab, alog_row, dtb_row = _gdn_gate_layout(w_in, gdn_a_log[j], gdn_dt_bias[j])
            qkvz, gcol, grow = _gdn_in(x, mod, n1, w_in[:, :4 * D_MODEL].astype(bf16), w_ab,
                                       gdn_conv_w[j], alog_row, dtb_row, l)
            ng = gdn_norm_g[j].reshape(1, GDN_DK)
            og, state_out = _gdn_core(qkvz, gcol, grow, ng, seq=SEQ, tile0=0, n_tiles=prompt_tiles,
                                      og_prev=None, state_prev=state_out, state_layer=j, emit_state=True)
            og, = _gdn_core(qkvz, gcol, grow, ng, seq=DEC_SEQ, tile0=prompt_tiles, n_tiles=DEC_BATCH,
                            og_prev=og, s0=state_gdn, s0_layer=j)
            w_out = gdn_w_out[j]
        else:
            qkv = _na_in(x, mod, n1, na_w_qkv[j].astype(bf16), l)
            og, new_k, new_v = _ctx_attn(qkv, j, new_k, new_v)
            og = _na_attn(qkv, ck, cv, bias, og, j)
            w_out = na_w_out[j]
        up, taps, down = _ffn_weights(ffn_w_up[l], ffn_conv_w[l], ffn_conv_b[l], ffn_w_down[l])
        x = _post(x, og, mod, n2, w_out.astype(bf16), up, taps, down, fg, l, l == DEPTH - 1)

    y_prompt = x[:N_PROMPT].reshape(BATCH, SEQ, D_MODEL)
    y_sample = x[N_PROMPT:].reshape(DEC_BATCH, DEC_SEQ, D_MODEL)
    new_cache_k = new_k.reshape(BATCH, 2, SEQ, NA_HEADS, NA_DH)
    new_cache_v = new_v.reshape(BATCH, 2, SEQ, NA_HEADS, NA_DH)
    return (y_prompt, y_sample, state_out, new_cache_k, new_cache_v)
```

```python
import functools

import jax
import jax.numpy as jnp
import numpy as np
from jax import lax
from jax.experimental import pallas as pl
from jax.experimental.pallas import tpu as pltpu

f32 = jnp.float32
bf16 = jnp.bfloat16

D_MODEL = 1024
BATCH = 16
SEQ = 256
DEPTH = 4
DEC_BATCH = 8
DEC_SEQ = 1024
PAST_LEN = 256
GRID_W = 64
GRID_ROWS = DEC_SEQ // GRID_W
GDN_HEADS = 8
GDN_DK = 128
GDN_CHUNK = 64
NA_HEADS = 16
NA_DH = 64
NA_KR = 8
NA_KC = 16
D_FF = 2816
N_MOD = 6
EPS = 1e-6
NEG_INF = -1e30

N_PROMPT = BATCH * SEQ
N_SAMPLE = DEC_BATCH * DEC_SEQ
N_TOK = N_PROMPT + N_SAMPLE

TM = 1024
TP = 512
HALO = 16
FF_CHUNK = 256
N_FF_CHUNKS = D_FF // FF_CHUNK
HEAD_GROUP = 4
N_HEAD_GROUPS = GDN_HEADS // HEAD_GROUP
GATE_LANES = 128
PREP_CHUNKS = 4
NA_ROW_GROUP = 4
MOD_ROWS = 16
MOD_TN = 1536

VMEM_LIMIT = 56 * 1024 * 1024


def _silu(x):
    return x * jax.nn.sigmoid(x)


def _dot(a, b):
    return jnp.dot(a, b, preferred_element_type=f32)


def _dot_nt(a, b):
    return lax.dot_general(a, b, (((1,), (1,)), ((), ())), preferred_element_type=f32)


def _dot_tn(a, b):
    return lax.dot_general(a, b, (((0,), (0,)), ((), ())), preferred_element_type=f32)


def _rms(x):
    return x * lax.rsqrt(jnp.mean(x * x, axis=-1, keepdims=True) + EPS)


def _params(sem, vmem=VMEM_LIMIT):
    return pltpu.CompilerParams(dimension_semantics=sem, vmem_limit_bytes=vmem)


def _mod_kernel(cv_ref, w_ref, b_ref, o_ref):
    s = _silu(cv_ref[...]).astype(bf16)
    o_ref[0] = _dot(s, w_ref[0].astype(bf16)) + b_ref[0]


def _modulation(cvec, w_ada, b_ada):
    n_tiles = (N_MOD * D_MODEL) // MOD_TN
    return pl.pallas_call(
        _mod_kernel,
        grid=(DEPTH, n_tiles),
        in_specs=[
            pl.BlockSpec((MOD_ROWS, D_MODEL), lambda l, n: (0, 0)),
            pl.BlockSpec((1, D_MODEL, MOD_TN), lambda l, n: (l, 0, n)),
            pl.BlockSpec((1, 1, MOD_TN), lambda l, n: (l, 0, n)),
        ],
        out_specs=pl.BlockSpec((1, MOD_ROWS, MOD_TN), lambda l, n: (l, 0, n)),
        out_shape=jax.ShapeDtypeStruct((DEPTH, MOD_ROWS, N_MOD * D_MODEL), f32),
        compiler_params=_params(("arbitrary", "arbitrary")),
        name="modulation",
    )(cvec, w_ada, b_ada.reshape(DEPTH, 1, N_MOD * D_MODEL))


def _mod_row_tm(i):
    return jnp.maximum(i - (N_PROMPT // TM - 1), 0)


def _norm_mod(x, g, shift, scale):
    return (_rms(x) * g) * (1.0 + scale) + shift


def _na_in_kernel(x_ref, mod_ref, g_ref, w_ref, o16_ref, kv_ref, h_ref):
    i = pl.program_id(0)
    n = pl.program_id(1)

    @pl.when(n == 0)
    def _():
        m = mod_ref[0, 0]
        h_ref[...] = _norm_mod(x_ref[...], g_ref[0], m[0:1], m[1:2]).astype(bf16)

    p = _dot(h_ref[...], w_ref[...])
    o16_ref[...] = p.astype(bf16)

    @pl.when((i < N_PROMPT // TM) & (n > 0))
    def _():
        kv_ref[...] = p


def _na_in(x, mod, norm_g, w, layer):
    n_out = w.shape[1] // D_MODEL
    prompt_tiles = N_PROMPT // TM

    def kv_map(i, n):
        return (jnp.minimum(i, prompt_tiles - 1), jnp.where(i < prompt_tiles, jnp.maximum(n - 1, 0), 1))

    return pl.pallas_call(
        _na_in_kernel,
        grid=(N_TOK // TM, n_out),
        in_specs=[
            pl.BlockSpec((TM, D_MODEL), lambda i, n: (i, 0)),
            pl.BlockSpec((1, 1, N_MOD, D_MODEL), lambda i, n: (layer, _mod_row_tm(i), 0, 0)),
            pl.BlockSpec((1, 1, D_MODEL), lambda i, n: (layer, 0, 0)),
            pl.BlockSpec((D_MODEL, D_MODEL), lambda i, n: (0, n)),
        ],
        out_specs=[pl.BlockSpec((TM, D_MODEL), lambda i, n: (i, n)),
                   pl.BlockSpec((TM, D_MODEL), kv_map)],
        out_shape=[jax.ShapeDtypeStruct((N_TOK, w.shape[1]), bf16),
                   jax.ShapeDtypeStruct((N_PROMPT, 2 * D_MODEL), f32)],
        scratch_shapes=[pltpu.VMEM((TM, D_MODEL), bf16)],
        compiler_params=_params(("arbitrary", "arbitrary")),
        name="na_in",
    )(x, mod, norm_g, w)


def _seq_len_of_tile(i, tile):
    return jnp.where(i < N_PROMPT // tile, SEQ, DEC_SEQ)


def _gdn_in_kernel(x_ref, mod_ref, g_ref, w_ref, wab_ref, cw_ref, alog_ref, dtb_ref,
                   o_ref, gcol_ref, grow_ref, h_ref):
    i = pl.program_id(0)
    n = pl.program_id(1)

    @pl.when(n == 0)
    def _():
        m = mod_ref[0, 0]
        h_ref[...] = _norm_mod(x_ref[...], g_ref[0], m[0:1], m[1:2]).astype(bf16)

    def conv_silu():
        p = _dot(h_ref[...], w_ref[...])
        seq = _seq_len_of_tile(i, TM)
        pos = lax.broadcasted_iota(jnp.int32, (TM, 1), 0) & (seq - 1)
        prev = jnp.where(pos == 0, 0.0, pltpu.roll(p, 1, axis=0))
        nxt = jnp.where(pos == seq - 1, 0.0, pltpu.roll(p, TM - 1, axis=0))
        cw = cw_ref[...]
        return _silu(cw[0:1] * prev + cw[1:2] * p + cw[2:3] * nxt)

    @pl.when(n < 2)
    def _():
        s = conv_silu()
        qscale = jnp.where(n == 0, GDN_DK ** -0.5, 1.0).astype(f32)
        for h in range(GDN_HEADS):
            sh = s[:, h * GDN_DK:(h + 1) * GDN_DK]
            nh = sh * lax.rsqrt(jnp.sum(sh * sh, axis=-1, keepdims=True) + EPS)
            o_ref[:, h * GDN_DK:(h + 1) * GDN_DK] = nh * qscale

    @pl.when(n == 2)
    def _():
        o_ref[...] = conv_silu()

    @pl.when(n == 3)
    def _():
        o_ref[...] = _dot(h_ref[...], w_ref[...])

    @pl.when(n == 4)
    def _():
        ab = _dot(h_ref[...], wab_ref[...])
        width = ab.shape[1]
        kind = lax.broadcasted_iota(jnp.int32, (1, width), 1) & 7
        xs = ab + dtb_ref[...]
        softplus = jnp.maximum(xs, 0.0) + jnp.log1p(jnp.exp(-jnp.abs(xs)))
        g = -jnp.exp(alog_ref[...]) * softplus
        pos = lax.broadcasted_iota(jnp.int32, (TM, 1), 0) & (GDN_CHUNK - 1)
        cf = g
        cb = g
        s = 1
        while s < GDN_CHUNK:
            cf = cf + jnp.where(pos >= s, pltpu.roll(cf, s, axis=0), 0.0)
            cb = cb + jnp.where(pos < GDN_CHUNK - s, pltpu.roll(cb, TM - s, axis=0), 0.0)
            s *= 2
        gates = jnp.where(kind == 0, cf, jnp.where(kind == 1, cb, jnp.where(kind < 4, jax.nn.sigmoid(ab), 0.0)))
        gcol_ref[...] = gates
        for c in range(TM // GDN_CHUNK):
            for hg in range(width // GATE_LANES):
                bt = gates[c * GDN_CHUNK:(c + 1) * GDN_CHUNK, hg * GATE_LANES:(hg + 1) * GATE_LANES].T
                grow_ref[c, hg * GATE_LANES:(hg + 1) * GATE_LANES, :] = jnp.concatenate([bt, bt], axis=1)


def _gdn_in(x, mod, norm_g, w_qkvz, w_ab, conv_w, alog_row, dtb_row, layer):
    n_chunks = N_TOK // GDN_CHUNK
    gw = N_HEAD_GROUPS * GATE_LANES
    n_proj = w_qkvz.shape[1] // D_MODEL
    return pl.pallas_call(
        _gdn_in_kernel,
        grid=(N_TOK // TM, n_proj + 1),
        in_specs=[
            pl.BlockSpec((TM, D_MODEL), lambda i, n: (i, 0)),
            pl.BlockSpec((1, 1, N_MOD, D_MODEL), lambda i, n: (layer, _mod_row_tm(i), 0, 0)),
            pl.BlockSpec((1, 1, D_MODEL), lambda i, n: (layer, 0, 0)),
            pl.BlockSpec((D_MODEL, D_MODEL), lambda i, n: (0, jnp.minimum(n, n_proj - 1))),
            pl.BlockSpec((D_MODEL, gw), lambda i, n: (0, 0)),
            pl.BlockSpec((3, D_MODEL), lambda i, n: (0, jnp.minimum(n, 2))),
            pl.BlockSpec((1, gw), lambda i, n: (0, 0)),
            pl.BlockSpec((1, gw), lambda i, n: (0, 0)),
        ],
        out_specs=[
            pl.BlockSpec((TM, D_MODEL), lambda i, n: (i, jnp.minimum(n, n_proj - 1))),
            pl.BlockSpec((TM, gw), lambda i, n: (i, 0)),
            pl.BlockSpec((TM // GDN_CHUNK, gw, 2 * GDN_CHUNK), lambda i, n: (i, 0, 0)),
        ],
        out_shape=[
            jax.ShapeDtypeStruct((N_TOK, w_qkvz.shape[1]), f32),
            jax.ShapeDtypeStruct((N_TOK, gw), f32),
            jax.ShapeDtypeStruct((n_chunks, gw, 2 * GDN_CHUNK), f32),
        ],
        scratch_shapes=[pltpu.VMEM((TM, D_MODEL), bf16)],
        compiler_params=_params(("arbitrary", "arbitrary")),
        name="gdn_in",
    )(x, mod, norm_g, w_qkvz, w_ab, conv_w, alog_row, dtb_row)


def _block_diag(x, left):
    return jnp.concatenate([jnp.where(left, x, 0.0), jnp.where(left, 0.0, x)], axis=0).astype(bf16)


def _tri_inverse_pairs(a_list, eye, level_masks, left):
    ts = [eye - jnp.where(level_masks[0], a, 0.0) for a in a_list]
    for mask in level_masks[1:]:
        tls = [_dot(t.astype(bf16), _block_diag(jnp.where(mask, a, 0.0), left)) for t, a in zip(ts, a_list)]
        ts = [t - _dot(tl.astype(bf16), _block_diag(t, left)) for t, tl in zip(ts, tls)]
    return ts


def _gdn_core_kernel(*refs, seq, has_s0, emit_state):
    q_ref, k_ref, v_ref, z_ref, gcol_ref, grow_ref, ng_ref = refs[:7]
    pos = 7
    s0_ref = None
    if has_s0:
        s0_ref = refs[pos]
        pos += 1
    o_ref = refs[pos]
    pos += 1
    sfin_ref = None
    if emit_state:
        sfin_ref = refs[pos]
        pos += 1
    u_s, wq_s, kt_s, in_s, eg_s, st_s, oo_s = refs[pos:]

    C = GDN_CHUNK
    n_seq = TM // seq
    chunks_per_seq = seq // C
    ii = lax.broadcasted_iota(jnp.int32, (C, 2 * C), 0)
    lane = lax.broadcasted_iota(jnp.int32, (C, 2 * C), 1)
    jj = lane & (C - 1)
    left = lane < C
    eye = (ii == jj).astype(f32)
    incl = (left & (ii >= jj)) | (~left & (ii <= jj))
    strict = incl & (ii != jj)
    level_masks = []
    sh = 0
    while (1 << sh) < C:
        level_masks.append(((ii >> (sh + 1)) == (jj >> (sh + 1))) & ((ii >> sh) != (jj >> sh)))
        sh += 1
    zero_rhs = jnp.zeros((C, 2 * GDN_DK), bf16)

    def prep(it, carry):
        units = []
        for cc in range(PREP_CHUNKS):
            c = it * PREP_CHUNKS + cc
            rows = pl.ds(pl.multiple_of(c * C, C), C)
            gc_tile = gcol_ref[rows, :]
            gr_tile = grow_ref[c]
            for hl in range(HEAD_GROUP):
                ls = slice(hl * GDN_DK, (hl + 1) * GDN_DK)
                kc = k_ref[rows, ls]
                qc = q_ref[rows, ls]
                k16 = kc.astype(bf16)
                kk16 = jnp.concatenate([k16, k16], axis=0)
                l0 = 8 * hl
                units.append(dict(
                    c=c, rows=rows, hl=hl, kc=kc, qc=qc, vc=v_ref[rows, ls],
                    gram=_dot_nt(k16, kk16), qk=_dot_nt(qc.astype(bf16), kk16),
                    gcol=(gc_tile[:, l0:l0 + 1], gc_tile[:, l0 + 1:l0 + 2]),
                    bcol=(gc_tile[:, l0 + 2:l0 + 3], gc_tile[:, l0 + 3:l0 + 4]),
                    grow=jnp.where(left[0:1], gr_tile[l0:l0 + 1, :], gr_tile[l0 + 1:l0 + 2, :])))
        for un in units:
            diff = jnp.where(left, un["gcol"][0], un["gcol"][1]) - un["grow"]
            un["decay"] = jnp.where(incl, jnp.exp(jnp.where(incl, diff, 0.0)), 0.0)
            bpair = jnp.where(left, un["bcol"][0], un["bcol"][1])
            un["a"] = jnp.where(strict, un["gram"] * un["decay"] * bpair, 0.0)
        ts = _tri_inverse_pairs([un["a"] for un in units], eye, level_masks, left)
        for un, t in zip(units, ts):
            un["eg"] = [jnp.exp(g) for g in un["gcol"]]
            halves = []
            for d in range(2):
                b = un["bcol"][d]
                halves.append(jnp.concatenate([(un["vc"] * b).astype(bf16),
                                               (un["kc"] * (b * un["eg"][d])).astype(bf16)], axis=1))
            rhs = jnp.concatenate([jnp.concatenate([halves[0], zero_rhs], axis=1),
                                   jnp.concatenate([zero_rhs, halves[1]], axis=1)], axis=0)
            un["uw"] = _dot(t.astype(bf16), rhs)
        for un in units:
            c = un["c"]
            rows = un["rows"]
            intra = un["qk"] * un["decay"]
            for d in range(2):
                idx = un["hl"] * 2 + d
                gcol = un["gcol"][d]
                glast = gcol[C - 1:C] if d == 0 else gcol[0:1]
                base = 2 * d * GDN_DK
                u_s[idx, rows, :] = un["uw"][:, base:base + GDN_DK]
                wq_s[idx, c, 0:C, :] = un["uw"][:, base + GDN_DK:base + 2 * GDN_DK].astype(bf16)
                wq_s[idx, c, C:2 * C, :] = (un["qc"] * un["eg"][d]).astype(bf16)
                kt_s[idx, rows, :] = (un["kc"] * jnp.exp(glast - gcol)).astype(bf16)
                in_s[idx, c] = jnp.where(left if d == 0 else ~left, intra, 0.0).astype(bf16)
                eg_s[idx, c] = jnp.broadcast_to(jnp.exp(glast), (8, GDN_DK))
        return carry

    lax.fori_loop(0, TM // C // PREP_CHUNKS, prep, 0)

    lanes = []
    for si in range(n_seq):
        for hl in range(HEAD_GROUP):
            for d in range(2):
                lanes.append((si, hl, d))
    for li, (si, hl, d) in enumerate(lanes):
        st_s[li] = s0_ref[si, 0, d, hl] if has_s0 else jnp.zeros((GDN_DK, GDN_DK), f32)

    def scan(t, carry):
        work = []
        for li, (si, hl, d) in enumerate(lanes):
            c = si * chunks_per_seq + (t if d == 0 else chunks_per_seq - 1 - t)
            rows = pl.ds(pl.multiple_of(c * C, C), C)
            work.append(dict(li=li, idx=hl * 2 + d, hl=hl, d=d, c=c, rows=rows, s=st_s[li]))
        for w in work:
            w["wsqs"] = _dot(wq_s[w["idx"], w["c"]], w["s"].astype(bf16))
        for w in work:
            v16 = (u_s[w["idx"], w["rows"], :] - w["wsqs"][:C]).astype(bf16)
            w["o"] = w["wsqs"][C:] + _dot(in_s[w["idx"], w["c"]], jnp.concatenate([v16, v16], axis=0))
            w["ds"] = _dot_tn(kt_s[w["idx"], w["rows"], :], v16)
        for w in work:
            ls = slice(w["hl"] * GDN_DK, (w["hl"] + 1) * GDN_DK)
            oo_s[w["d"], w["rows"], ls] = w["o"]
            st_s[w["li"]] = w["s"] * eg_s[w["idx"], w["c"]][0:1, :] + w["ds"]
        return carry

    lax.fori_loop(0, chunks_per_seq, scan, 0)

    ng = ng_ref[...]
    for hl in range(HEAD_GROUP):
        ls = slice(hl * GDN_DK, (hl + 1) * GDN_DK)
        o = oo_s[0, :, ls] + oo_s[1, :, ls]
        o_ref[:, ls] = (_rms(o) * ng * _silu(z_ref[:, ls])).astype(bf16)
    if emit_state:
        for li, (si, hl, d) in enumerate(lanes):
            sfin_ref[si, d, hl] = st_s[li]


def _gdn_core(qkvz, gcol, grow, norm_g_row, *, seq, tile0, n_tiles, s0=None, s0_layer=0, emit_state=False):
    hw = HEAD_GROUP * GDN_DK
    n_seq = TM // seq
    n_chunks = TM // GDN_CHUNK
    n_ch = HEAD_GROUP * 2

    def col_spec(base):
        return pl.BlockSpec((TM, hw), lambda s, hg: (tile0 + s, base * N_HEAD_GROUPS + hg))

    in_specs = [col_spec(0), col_spec(1), col_spec(2), col_spec(3),
                pl.BlockSpec((TM, GATE_LANES), lambda s, hg: (tile0 + s, hg)),
                pl.BlockSpec((n_chunks, GATE_LANES, 2 * GDN_CHUNK), lambda s, hg: (tile0 + s, hg, 0)),
                pl.BlockSpec((1, GDN_DK), lambda s, hg: (0, 0))]
    args = [qkvz, qkvz, qkvz, qkvz, gcol, grow, norm_g_row]
    if s0 is not None:
        in_specs.append(pl.BlockSpec((n_seq, 1, 2, HEAD_GROUP, GDN_DK, GDN_DK),
                                     lambda s, hg: (s, s0_layer, 0, hg, 0, 0)))
        args.append(s0)
    out_specs = [pl.BlockSpec((TM, hw), lambda s, hg: (s, hg))]
    out_shape = [jax.ShapeDtypeStruct((n_tiles * TM, D_MODEL), bf16)]
    if emit_state:
        out_specs.append(pl.BlockSpec((n_seq, 2, HEAD_GROUP, GDN_DK, GDN_DK), lambda s, hg: (s, 0, hg, 0, 0)))
        out_shape.append(jax.ShapeDtypeStruct((n_tiles * n_seq, 2, GDN_HEADS, GDN_DK, GDN_DK), f32))

    return pl.pallas_call(
        functools.partial(_gdn_core_kernel, seq=seq, has_s0=s0 is not None, emit_state=emit_state),
        grid=(n_tiles, N_HEAD_GROUPS),
        in_specs=in_specs,
        out_specs=out_specs,
        out_shape=out_shape,
        scratch_shapes=[
            pltpu.VMEM((n_ch, TM, GDN_DK), f32),
            pltpu.VMEM((n_ch, n_chunks, 2 * GDN_CHUNK, GDN_DK), bf16),
            pltpu.VMEM((n_ch, TM, GDN_DK), bf16),
            pltpu.VMEM((n_ch, n_chunks, GDN_CHUNK, 2 * GDN_CHUNK), bf16),
            pltpu.VMEM((n_ch, n_chunks, 8, GDN_DK), f32),
            pltpu.VMEM((n_seq * n_ch, GDN_DK, GDN_DK), f32),
            pltpu.VMEM((2, TM, hw), f32),
        ],
        compiler_params=_params(("arbitrary", "arbitrary")),
        name="gdn_core_%d" % seq,
    )(*args)


def _ctx_attn_kernel(q_ref, k_ref, v_ref, o_ref, ko_ref, vo_ref):
    k = k_ref[...]
    v = v_ref[...]
    ko_ref[...] = k
    vo_ref[...] = v
    heads = []
    for h in range(NA_HEADS):
        hs = slice(h * NA_DH, (h + 1) * NA_DH)
        heads.append(dict(s=_dot_nt(q_ref[:, hs] * (NA_DH ** -0.5), k[:, hs].astype(bf16)),
                          v16=v[:, hs].astype(bf16)))
    for hd in heads:
        e = jnp.exp(hd["s"] - jnp.max(hd["s"], axis=-1, keepdims=True))
        hd["p16"] = (e / jnp.sum(e, axis=-1, keepdims=True)).astype(bf16)
    outs = [_dot(hd["p16"], hd["v16"]) for hd in heads]
    for hp in range(NA_HEADS // 2):
        o_ref[:, 2 * hp * NA_DH:(2 * hp + 2) * NA_DH] = jnp.concatenate(
            outs[2 * hp:2 * hp + 2], axis=1).astype(bf16)


def _ctx_attn(qkv16, kv32):
    cache_shape = jax.ShapeDtypeStruct((N_PROMPT, D_MODEL), f32)
    row_spec = pl.BlockSpec((SEQ, D_MODEL), lambda b: (b, 0))
    return pl.pallas_call(
        _ctx_attn_kernel,
        grid=(BATCH,),
        in_specs=[row_spec, row_spec, pl.BlockSpec((SEQ, D_MODEL), lambda b: (b, 1))],
        out_specs=[row_spec, row_spec, row_spec],
        out_shape=[jax.ShapeDtypeStruct((N_PROMPT, D_MODEL), bf16), cache_shape, cache_shape],
        compiler_params=_params(("arbitrary",)),
        name="ctx_attn",
    )(qkv16, kv32, kv32)


def _window_row_start(r):
    return min(max(r - NA_KR // 2, 0), GRID_ROWS - NA_KR)


def _na_attn_kernel(q_ref, k_ref, v_ref, ck_ref, cv_ref, bias_ref, o_ref):
    kk = [k_ref[:, hh * NA_DH:(hh + 1) * NA_DH] for hh in range(2)]
    vv = [v_ref[:, hh * NA_DH:(hh + 1) * NA_DH] for hh in range(2)]
    ck = [ck_ref[0, 0, :, hh * NA_DH:(hh + 1) * NA_DH].astype(bf16) for hh in range(2)]
    cv = [cv_ref[0, 0, :, hh * NA_DH:(hh + 1) * NA_DH].astype(bf16) for hh in range(2)]
    for r0 in range(0, GRID_ROWS, NA_ROW_GROUP):
        units = []
        for r in range(r0, r0 + NA_ROW_GROUP):
            rs = _window_row_start(r)
            for hh in range(2):
                q = q_ref[r * GRID_W:(r + 1) * GRID_W, hh * NA_DH:(hh + 1) * NA_DH] * (NA_DH ** -0.5)
                units.append(dict(r=r, hh=hh, d0=rs - r + NA_KR - 1, q=q,
                                  win=slice(rs * GRID_W, (rs + NA_KR) * GRID_W)))
        for un in units:
            hh = un["hh"]
            un["s_loc"] = _dot_nt(un["q"], kk[hh][un["win"]]) + bias_ref[0, hh, un["d0"]]
            un["s_ctx"] = _dot_nt(un["q"], ck[hh])
        for un in units:
            m = jnp.maximum(jnp.max(un["s_loc"], axis=-1, keepdims=True),
                            jnp.max(un["s_ctx"], axis=-1, keepdims=True))
            e_loc = jnp.exp(un["s_loc"] - m)
            e_ctx = jnp.exp(un["s_ctx"] - m)
            un["den"] = jnp.sum(e_loc, axis=-1, keepdims=True) + jnp.sum(e_ctx, axis=-1, keepdims=True)
            un["e_loc"] = e_loc.astype(bf16)
            un["e_ctx"] = e_ctx.astype(bf16)
        for un in units:
            hh = un["hh"]
            un["o"] = (_dot(un["e_loc"], vv[hh][un["win"]]) + _dot(un["e_ctx"], cv[hh])) / un["den"]
        for k in range(NA_ROW_GROUP):
            r = r0 + k
            o_ref[r * GRID_W:(r + 1) * GRID_W, :] = jnp.concatenate(
                [units[2 * k]["o"], units[2 * k + 1]["o"]], axis=1).astype(bf16)


def _na_attn(qkv16, cache_k, cache_v, bias, layer_slot):
    pw = 2 * NA_DH
    n_pairs = NA_HEADS // 2
    rb0 = N_PROMPT // DEC_SEQ
    return pl.pallas_call(
        _na_attn_kernel,
        grid=(n_pairs, DEC_BATCH),
        in_specs=[
            pl.BlockSpec((DEC_SEQ, pw), lambda hp, b: (rb0 + b, hp)),
            pl.BlockSpec((DEC_SEQ, pw), lambda hp, b: (rb0 + b, n_pairs + hp)),
            pl.BlockSpec((DEC_SEQ, pw), lambda hp, b: (rb0 + b, 2 * n_pairs + hp)),
            pl.BlockSpec((1, 1, PAST_LEN, pw), lambda hp, b: (b, layer_slot, 0, hp)),
            pl.BlockSpec((1, 1, PAST_LEN, pw), lambda hp, b: (b, layer_slot, 0, hp)),
            pl.BlockSpec((1, 2, NA_KR, GRID_W, NA_KR * GRID_W), lambda hp, b: (layer_slot, hp, 0, 0, 0)),
        ],
        out_specs=pl.BlockSpec((DEC_SEQ, pw), lambda hp, b: (b, hp)),
        out_shape=jax.ShapeDtypeStruct((N_SAMPLE, D_MODEL), bf16),
        compiler_params=_params(("arbitrary", "arbitrary")),
        name="na_attn",
    )(qkv16, qkv16, qkv16, cache_k, cache_v, bias)


def _na_bias_tiles(rel_bias):
    col = np.arange(GRID_W)
    col_start = np.clip(col - NA_KC // 2, 0, GRID_W - NA_KC)
    col_mask = (col[None, :] >= col_start[:, None]) & (col[None, :] < col_start[:, None] + NA_KC)
    col_idx = np.clip(col[None, :] - col[:, None] + NA_KC - 1, 0, 2 * NA_KC - 2)
    g = rel_bias[:, :, :, col_idx]
    g = jnp.where(col_mask[None, None, None], g, NEG_INF)
    tiles = [jnp.transpose(g[:, :, d0:d0 + NA_KR], (0, 1, 3, 2, 4)).reshape(
        g.shape[0], NA_HEADS, GRID_W, NA_KR * GRID_W) for d0 in range(NA_KR)]
    return jnp.stack(tiles, axis=2)


def _post_kernel(x_ref, xp_ref, xn_ref, op_ref, opp_ref, opn_ref, os_ref, osp_ref, osn_ref, mod_ref, g_ref,
                 wout_ref, wup_ref, cw_ref, wdn_ref, fg_ref, *rest, final):
    if final:
        yp_ref, ys_ref, xe_s, oe_s, he_s, acc_s = rest
    else:
        out_ref, xe_s, oe_s, he_s, acc_s = rest
    i = pl.program_id(0)
    prompt_tiles = N_PROMPT // TP
    ext = TP + 2 * HALO
    xe_s[0:HALO] = xp_ref[...]
    xe_s[HALO:HALO + TP] = x_ref[...]
    xe_s[HALO + TP:ext] = xn_ref[...]

    @pl.when(i < prompt_tiles)
    def _():
        oe_s[0:HALO] = opp_ref[...]
        oe_s[HALO:HALO + TP] = op_ref[...]
        oe_s[HALO + TP:ext] = opn_ref[...]

    @pl.when(i >= prompt_tiles)
    def _():
        oe_s[0:HALO] = osp_ref[...]
        oe_s[HALO:HALO + TP] = os_ref[...]
        oe_s[HALO + TP:ext] = osn_ref[...]

    m = mod_ref[0, 0]
    x1 = xe_s[...] + m[2:3] * _dot(oe_s[...], wout_ref[...])
    xe_s[...] = x1
    he_s[...] = _norm_mod(x1, g_ref[0], m[3:4], m[4:5]).astype(bf16)

    seq = _seq_len_of_tile(i, TP)
    pos = (lax.broadcasted_iota(jnp.int32, (TP, 1), 0) + i * TP) & (seq - 1)
    first = pos == 0
    last = pos == seq - 1

    for j in range(N_FF_CHUNKS):
        up = _dot(he_s[...], wup_ref[j])
        cw = cw_ref[j]
        prev = jnp.where(first, 0.0, pltpu.roll(up, 1, axis=0)[HALO:HALO + TP])
        nxt = jnp.where(last, 0.0, pltpu.roll(up, ext - 1, axis=0)[HALO:HALO + TP])
        u = cw[0:1] * prev + cw[1:2] * up[HALO:HALO + TP] + cw[2:3] * nxt + cw[3:4]
        act = (_silu(u[:, FF_CHUNK:]) * u[:, :FF_CHUNK]).astype(bf16)
        down = _dot(act, wdn_ref[j])
        if j == 0:
            acc_s[...] = down
        else:
            acc_s[...] += down

    y = xe_s[HALO:HALO + TP] + m[5:6] * acc_s[...]
    if final:
        y = _rms(y) * fg_ref[...]

        @pl.when(i < prompt_tiles)
        def _():
            yp_ref[...] = y

        @pl.when(i >= prompt_tiles)
        def _():
            ys_ref[...] = y
    else:
        out_ref[...] = y


def _post(x, og_p, og_s, mod, norm_g, w_out, w_up, conv_wb, w_down, final_g, layer, final):
    n_tiles = N_TOK // TP
    prompt_tiles = N_PROMPT // TP
    hb = TP // HALO

    def halo_maps(tile_of, n_rows):
        n_hb = n_rows // HALO
        last_tile = n_rows // TP - 1

        def main(i):
            return (jnp.clip(tile_of(i), 0, last_tile), 0)

        def prev(i):
            return (jnp.clip(tile_of(i) * hb - 1, 0, n_hb - 1), 0)

        def nxt(i):
            return (jnp.clip((tile_of(i) + 1) * hb, 0, n_hb - 1), 0)

        return main, prev, nxt

    x_maps = halo_maps(lambda i: i, N_TOK)
    p_maps = halo_maps(lambda i: i, N_PROMPT)
    s_maps = halo_maps(lambda i: i - prompt_tiles, N_SAMPLE)

    def mod_map(i):
        tiles_per_req = DEC_SEQ // TP
        return (layer, jnp.maximum((i - (prompt_tiles - tiles_per_req)) // tiles_per_req, 0), 0, 0)

    def triple(maps):
        return [pl.BlockSpec((TP, D_MODEL), maps[0]), pl.BlockSpec((HALO, D_MODEL), maps[1]),
                pl.BlockSpec((HALO, D_MODEL), maps[2])]

    resident = dict(pipeline_mode=pl.Buffered(1))
    ext = TP + 2 * HALO
    if final:
        out_specs = [pl.BlockSpec((TP, D_MODEL), p_maps[0]), pl.BlockSpec((TP, D_MODEL), s_maps[0])]
        out_shape = [jax.ShapeDtypeStruct((N_PROMPT, D_MODEL), f32), jax.ShapeDtypeStruct((N_SAMPLE, D_MODEL), f32)]
    else:
        out_specs = pl.BlockSpec((TP, D_MODEL), lambda i: (i, 0))
        out_shape = jax.ShapeDtypeStruct((N_TOK, D_MODEL), f32)
    return pl.pallas_call(
        functools.partial(_post_kernel, final=final),
        grid=(n_tiles,),
        in_specs=triple(x_maps) + triple(p_maps) + triple(s_maps) + [
            pl.BlockSpec((1, 1, N_MOD, D_MODEL), mod_map),
            pl.BlockSpec((1, 1, D_MODEL), lambda i: (layer, 0, 0)),
            pl.BlockSpec((D_MODEL, D_MODEL), lambda i: (0, 0), **resident),
            pl.BlockSpec((N_FF_CHUNKS, D_MODEL, 2 * FF_CHUNK), lambda i: (0, 0, 0), **resident),
            pl.BlockSpec((N_FF_CHUNKS, 8, 2 * FF_CHUNK), lambda i: (0, 0, 0), **resident),
            pl.BlockSpec((N_FF_CHUNKS, FF_CHUNK, D_MODEL), lambda i: (0, 0, 0), **resident),
            pl.BlockSpec((1, D_MODEL), lambda i: (0, 0)),
        ],
        out_specs=out_specs,
        out_shape=out_shape,
        scratch_shapes=[
            pltpu.VMEM((ext, D_MODEL), f32),
            pltpu.VMEM((ext, D_MODEL), bf16),
            pltpu.VMEM((ext, D_MODEL), bf16),
            pltpu.VMEM((TP, D_MODEL), f32),
        ],
        compiler_params=_params(("arbitrary",)),
        name="post",
    )(x, x, x, og_p, og_p, og_p, og_s, og_s, og_s, mod, norm_g, w_out, w_up, conv_wb, w_down, final_g)


def _ffn_weights(w_up, conv_w, conv_b, w_down):
    def pair(t):
        val = t[..., :D_FF].reshape(t.shape[:-1] + (N_FF_CHUNKS, FF_CHUNK))
        gate = t[..., D_FF:].reshape(t.shape[:-1] + (N_FF_CHUNKS, FF_CHUNK))
        return jnp.moveaxis(jnp.concatenate([val, gate], axis=-1), -2, 0)

    up = pair(w_up).astype(bf16)
    taps = pair(jnp.concatenate([conv_w, conv_b[None], jnp.zeros((4, 2 * D_FF), f32)], axis=0))
    down = w_down.reshape(N_FF_CHUNKS, FF_CHUNK, D_MODEL).astype(bf16)
    return up, taps, down


def _gdn_gate_layout(w_in, a_log, dt_bias):
    base = 4 * D_MODEL
    gw = N_HEAD_GROUPS * GATE_LANES
    src = np.zeros((gw,), np.int32)
    used = np.zeros((gw,), bool)
    gate_src = np.zeros((gw,), np.int32)
    is_a = np.zeros((gw,), bool)
    for h in range(GDN_HEADS):
        hg, hl = divmod(h, HEAD_GROUP)
        for kind in range(4):
            lane = hg * GATE_LANES + 8 * hl + kind
            d = kind % 2
            src[lane] = base + (kind // 2) * 2 * GDN_HEADS + d * GDN_HEADS + h
            used[lane] = True
            gate_src[lane] = d * GDN_HEADS + h
            is_a[lane] = kind < 2
    w_ab = jnp.where(used[None, :], w_in[:, src], 0.0).astype(bf16)
    alog_row = jnp.where(is_a, a_log.reshape(-1)[gate_src], 0.0)[None, :]
    dtb_row = jnp.where(is_a, dt_bias.reshape(-1)[gate_src], 0.0)[None, :]
    return w_ab, alog_row.astype(f32), dtb_row.astype(f32)


def kernel(x_prompt, x_sample, state_gdn, cache_k, cache_v, c, c_ctx, w_ada, b_ada, norm1_g, norm2_g,
           gdn_w_in, gdn_conv_w, gdn_a_log, gdn_dt_bias, gdn_norm_g, gdn_w_out,
           na_w_qkv, na_rel_bias, na_w_out, ffn_w_up, ffn_conv_w, ffn_conv_b, ffn_w_down, final_g):
    x = jnp.concatenate([x_prompt.reshape(N_PROMPT, D_MODEL), x_sample.reshape(N_SAMPLE, D_MODEL)], axis=0)
    cvec = jnp.concatenate([c_ctx[None, :], c, jnp.zeros((MOD_ROWS - 1 - DEC_BATCH, D_MODEL), f32)], axis=0)
    mod = _modulation(cvec, w_ada, b_ada).reshape(DEPTH, MOD_ROWS, N_MOD, D_MODEL)
    n1 = norm1_g.reshape(DEPTH, 1, D_MODEL)
    n2 = norm2_g.reshape(DEPTH, 1, D_MODEL)
    fg = final_g.reshape(1, D_MODEL)
    ck = cache_k.reshape(DEC_BATCH, 2, PAST_LEN, D_MODEL)
    cv = cache_v.reshape(DEC_BATCH, 2, PAST_LEN, D_MODEL)
    bias = _na_bias_tiles(na_rel_bias)
    prompt_tiles = N_PROMPT // TM

    states, new_ks, new_vs = [], [], []
    for l in range(DEPTH):
        j = l // 2
        if l % 2 == 0:
            w_in = gdn_w_in[j]
            w_ab, alog_row, dtb_row = _gdn_gate_layout(w_in, gdn_a_log[j], gdn_dt_bias[j])
            qkvz, gcol, grow = _gdn_in(x, mod, n1, w_in[:, :4 * D_MODEL].astype(bf16), w_ab,
                                       gdn_conv_w[j], alog_row, dtb_row, l)
            ng = gdn_norm_g[j].reshape(1, GDN_DK)
            og_p, st = _gdn_core(qkvz, gcol, grow, ng, seq=SEQ, tile0=0, n_tiles=prompt_tiles, emit_state=True)
            og_s, = _gdn_core(qkvz, gcol, grow, ng, seq=DEC_SEQ, tile0=prompt_tiles, n_tiles=DEC_BATCH,
                              s0=state_gdn, s0_layer=j)
            states.append(st)
            w_out = gdn_w_out[j]
        else:
            qkv16, kv32 = _na_in(x, mod, n1, na_w_qkv[j].astype(bf16), l)
            og_p, nk, nv = _ctx_attn(qkv16, kv32)
            og_s = _na_attn(qkv16, ck, cv, bias, j)
            new_ks.append(nk.reshape(BATCH, SEQ, NA_HEADS, NA_DH))
            new_vs.append(nv.reshape(BATCH, SEQ, NA_HEADS, NA_DH))
            w_out = na_w_out[j]
        up, taps, down = _ffn_weights(ffn_w_up[l], ffn_conv_w[l], ffn_conv_b[l], ffn_w_down[l])
        x = _post(x, og_p, og_s, mod, n2, w_out.astype(bf16), up, taps, down, fg, l, l == DEPTH - 1)

    y_prompt = x[0].reshape(BATCH, SEQ, D_MODEL)
    y_sample = x[1].reshape(DEC_BATCH, DEC_SEQ, D_MODEL)
    return (y_prompt, y_sample, jnp.stack(states, axis=1), jnp.stack(new_ks, axis=1), jnp.stack(new_vs, axis=1))
```

```python
import functools

import jax
import jax.numpy as jnp
import numpy as np
from jax import lax
from jax.experimental import pallas as pl
from jax.experimental.pallas import tpu as pltpu

f32 = jnp.float32
bf16 = jnp.bfloat16

D_MODEL = 1024
BATCH = 16
SEQ = 256
DEPTH = 4
DEC_BATCH = 8
DEC_SEQ = 1024
PAST_LEN = 256
GRID_W = 64
GRID_ROWS = DEC_SEQ // GRID_W
GDN_HEADS = 8
GDN_DK = 128
GDN_CHUNK = 64
NA_HEADS = 16
NA_DH = 64
NA_KR = 8
NA_KC = 16
D_FF = 2816
N_MOD = 6
EPS = 1e-6
NEG_INF = -1e30

N_PROMPT = BATCH * SEQ
N_SAMPLE = DEC_BATCH * DEC_SEQ
N_TOK = N_PROMPT + N_SAMPLE

TM = 1024
TP = 512
HALO = 16
FF_CHUNK = 512
HEAD_GROUP = 4
N_HEAD_GROUPS = GDN_HEADS // HEAD_GROUP
GATE_LANES = 128
PREP_CHUNKS = 4
NA_ROW_GROUP = 4
MOD_ROWS = 16
MOD_TN = 1536

VMEM_LIMIT = 56 * 1024 * 1024


def _silu(x):
    return x * jax.nn.sigmoid(x)


def _dot(a, b):
    return jnp.dot(a, b, preferred_element_type=f32)


def _dot_nt(a, b):
    return lax.dot_general(a, b, (((1,), (1,)), ((), ())), preferred_element_type=f32)


def _dot_tn(a, b):
    return lax.dot_general(a, b, (((0,), (0,)), ((), ())), preferred_element_type=f32)


def _rms(x):
    return x * lax.rsqrt(jnp.mean(x * x, axis=-1, keepdims=True) + EPS)


def _params(sem, vmem=VMEM_LIMIT):
    return pltpu.CompilerParams(dimension_semantics=sem, vmem_limit_bytes=vmem)


def _mod_kernel(cv_ref, w_ref, b_ref, o_ref):
    s = _silu(cv_ref[...]).astype(bf16)
    o_ref[0] = _dot(s, w_ref[0].astype(bf16)) + b_ref[0]


def _modulation(cvec, w_ada, b_ada):
    n_tiles = (N_MOD * D_MODEL) // MOD_TN
    return pl.pallas_call(
        _mod_kernel,
        grid=(DEPTH, n_tiles),
        in_specs=[
            pl.BlockSpec((MOD_ROWS, D_MODEL), lambda l, n: (0, 0)),
            pl.BlockSpec((1, D_MODEL, MOD_TN), lambda l, n: (l, 0, n)),
            pl.BlockSpec((1, 1, MOD_TN), lambda l, n: (l, 0, n)),
        ],
        out_specs=pl.BlockSpec((1, MOD_ROWS, MOD_TN), lambda l, n: (l, 0, n)),
        out_shape=jax.ShapeDtypeStruct((DEPTH, MOD_ROWS, N_MOD * D_MODEL), f32),
        compiler_params=_params(("arbitrary", "arbitrary")),
        name="modulation",
    )(cvec, w_ada, b_ada.reshape(DEPTH, 1, N_MOD * D_MODEL))


def _mod_row_tm(i):
    return jnp.maximum(i - (N_PROMPT // TM - 1), 0)


def _norm_mod(x, g, shift, scale):
    return (_rms(x) * g) * (1.0 + scale) + shift


def _na_in_kernel(x_ref, mod_ref, g_ref, w_ref, o16_ref, kv_ref, h_ref):
    i = pl.program_id(0)
    n = pl.program_id(1)

    @pl.when(n == 0)
    def _():
        m = mod_ref[0, 0]
        h_ref[...] = _norm_mod(x_ref[...], g_ref[0], m[0:1], m[1:2]).astype(bf16)

    p = _dot(h_ref[...], w_ref[0])
    o16_ref[...] = p.astype(bf16)

    @pl.when((i < N_PROMPT // TM) & (n > 0))
    def _():
        kv_ref[...] = p


def _na_in(x, mod, norm_g, w, layer):
    n_out = w.shape[2] // D_MODEL
    mixer_slot = layer // 2
    prompt_tiles = N_PROMPT // TM

    def kv_map(i, n):
        return (jnp.minimum(i, prompt_tiles - 1), jnp.where(i < prompt_tiles, jnp.maximum(n - 1, 0), 1))

    return pl.pallas_call(
        _na_in_kernel,
        grid=(N_TOK // TM, n_out),
        in_specs=[
            pl.BlockSpec((TM, D_MODEL), lambda i, n: (i, 0)),
            pl.BlockSpec((1, 1, N_MOD, D_MODEL), lambda i, n: (layer, _mod_row_tm(i), 0, 0)),
            pl.BlockSpec((1, 1, D_MODEL), lambda i, n: (layer, 0, 0)),
            pl.BlockSpec((1, D_MODEL, D_MODEL), lambda i, n: (mixer_slot, 0, n)),
        ],
        out_specs=[pl.BlockSpec((TM, D_MODEL), lambda i, n: (i, n)),
                   pl.BlockSpec((TM, D_MODEL), kv_map)],
        out_shape=[jax.ShapeDtypeStruct((N_TOK, w.shape[2]), bf16),
                   jax.ShapeDtypeStruct((N_PROMPT, 2 * D_MODEL), f32)],
        scratch_shapes=[pltpu.VMEM((TM, D_MODEL), bf16)],
        compiler_params=_params(("arbitrary", "arbitrary")),
        name="na_in",
    )(x, mod, norm_g, w)


def _seq_len_of_tile(i, tile):
    return jnp.where(i < N_PROMPT // tile, SEQ, DEC_SEQ)


def _gdn_in_kernel(x_ref, mod_ref, g_ref, w_ref, wab_ref, cw_ref, alog_ref, dtb_ref,
                   o_ref, gcol_ref, grow_ref, h_ref):
    i = pl.program_id(0)
    n = pl.program_id(1)

    @pl.when(n == 0)
    def _():
        m = mod_ref[0, 0]
        h_ref[...] = _norm_mod(x_ref[...], g_ref[0], m[0:1], m[1:2]).astype(bf16)

    def conv_silu():
        p = _dot(h_ref[...], w_ref[0])
        seq = _seq_len_of_tile(i, TM)
        pos = lax.broadcasted_iota(jnp.int32, (TM, 1), 0) & (seq - 1)
        prev = jnp.where(pos == 0, 0.0, pltpu.roll(p, 1, axis=0))
        nxt = jnp.where(pos == seq - 1, 0.0, pltpu.roll(p, TM - 1, axis=0))
        cw = cw_ref[0]
        return _silu(cw[0:1] * prev + cw[1:2] * p + cw[2:3] * nxt)

    @pl.when(n < 2)
    def _():
        s = conv_silu()
        qscale = jnp.where(n == 0, GDN_DK ** -0.5, 1.0).astype(f32)
        for h in range(GDN_HEADS):
            sh = s[:, h * GDN_DK:(h + 1) * GDN_DK]
            nh = sh * lax.rsqrt(jnp.sum(sh * sh, axis=-1, keepdims=True) + EPS)
            o_ref[:, h * GDN_DK:(h + 1) * GDN_DK] = nh * qscale

    @pl.when(n == 2)
    def _():
        o_ref[...] = conv_silu()

    @pl.when(n == 3)
    def _():
        o_ref[...] = _dot(h_ref[...], w_ref[0])

    @pl.when(n == 4)
    def _():
        ab = _dot(h_ref[...], wab_ref[...])
        width = ab.shape[1]
        kind = lax.broadcasted_iota(jnp.int32, (1, width), 1) & 7
        xs = ab + dtb_ref[...]
        softplus = jnp.maximum(xs, 0.0) + jnp.log1p(jnp.exp(-jnp.abs(xs)))
        g = -jnp.exp(alog_ref[...]) * softplus
        pos = lax.broadcasted_iota(jnp.int32, (TM, 1), 0) & (GDN_CHUNK - 1)
        cf = g
        cb = g
        s = 1
        while s < GDN_CHUNK:
            cf = cf + jnp.where(pos >= s, pltpu.roll(cf, s, axis=0), 0.0)
            cb = cb + jnp.where(pos < GDN_CHUNK - s, pltpu.roll(cb, TM - s, axis=0), 0.0)
            s *= 2
        gates = jnp.where(kind == 0, cf, jnp.where(kind == 1, cb, jnp.where(kind < 4, jax.nn.sigmoid(ab), 0.0)))
        gcol_ref[...] = gates
        for c in range(TM // GDN_CHUNK):
            for hg in range(width // GATE_LANES):
                bt = gates[c * GDN_CHUNK:(c + 1) * GDN_CHUNK, hg * GATE_LANES:(hg + 1) * GATE_LANES].T
                grow_ref[c, hg * GATE_LANES:(hg + 1) * GATE_LANES, :] = jnp.concatenate([bt, bt], axis=1)


def _gdn_in(x, mod, norm_g, w_in, w_ab, conv_w, alog_row, dtb_row, layer):
    n_chunks = N_TOK // GDN_CHUNK
    gw = N_HEAD_GROUPS * GATE_LANES
    n_proj = 4
    mixer_slot = layer // 2
    return pl.pallas_call(
        _gdn_in_kernel,
        grid=(N_TOK // TM, n_proj + 1),
        in_specs=[
            pl.BlockSpec((TM, D_MODEL), lambda i, n: (i, 0)),
            pl.BlockSpec((1, 1, N_MOD, D_MODEL), lambda i, n: (layer, _mod_row_tm(i), 0, 0)),
            pl.BlockSpec((1, 1, D_MODEL), lambda i, n: (layer, 0, 0)),
            pl.BlockSpec((1, D_MODEL, D_MODEL), lambda i, n: (mixer_slot, 0, jnp.minimum(n, n_proj - 1))),
            pl.BlockSpec((D_MODEL, gw), lambda i, n: (0, 0)),
            pl.BlockSpec((1, 3, D_MODEL), lambda i, n: (mixer_slot, 0, jnp.minimum(n, 2))),
            pl.BlockSpec((1, gw), lambda i, n: (0, 0)),
            pl.BlockSpec((1, gw), lambda i, n: (0, 0)),
        ],
        out_specs=[
            pl.BlockSpec((TM, D_MODEL), lambda i, n: (i, jnp.minimum(n, n_proj - 1))),
            pl.BlockSpec((TM, gw), lambda i, n: (i, 0)),
            pl.BlockSpec((TM // GDN_CHUNK, gw, 2 * GDN_CHUNK), lambda i, n: (i, 0, 0)),
        ],
        out_shape=[
            jax.ShapeDtypeStruct((N_TOK, n_proj * D_MODEL), f32),
            jax.ShapeDtypeStruct((N_TOK, gw), f32),
            jax.ShapeDtypeStruct((n_chunks, gw, 2 * GDN_CHUNK), f32),
        ],
        scratch_shapes=[pltpu.VMEM((TM, D_MODEL), bf16)],
        compiler_params=_params(("arbitrary", "arbitrary")),
        name="gdn_in",
    )(x, mod, norm_g, w_in, w_ab, conv_w, alog_row, dtb_row)


def _block_diag(x, left):
    return jnp.concatenate([jnp.where(left, x, 0.0), jnp.where(left, 0.0, x)], axis=0).astype(bf16)


def _tri_inverse_pairs(a_list, eye, level_masks, left):
    ts = [eye - jnp.where(level_masks[0], a, 0.0) for a in a_list]
    for mask in level_masks[1:]:
        tls = [_dot(t.astype(bf16), _block_diag(jnp.where(mask, a, 0.0), left)) for t, a in zip(ts, a_list)]
        ts = [t - _dot(tl.astype(bf16), _block_diag(t, left)) for t, tl in zip(ts, tls)]
    return ts


def _gdn_core_kernel(*refs, seq, has_s0, emit_state):
    q_ref, k_ref, v_ref, z_ref, gcol_ref, grow_ref, ng_ref = refs[:7]
    pos = 7
    s0_ref = None
    if has_s0:
        s0_ref = refs[pos]
        pos += 1
    o_ref = refs[pos]
    pos += 1
    sfin_ref = None
    if emit_state:
        sfin_ref = refs[pos]
        pos += 1
    u_s, wq_s, kt_s, in_s, eg_s, st_s, oo_s = refs[pos:]

    C = GDN_CHUNK
    n_seq = TM // seq
    chunks_per_seq = seq // C
    ii = lax.broadcasted_iota(jnp.int32, (C, 2 * C), 0)
    lane = lax.broadcasted_iota(jnp.int32, (C, 2 * C), 1)
    jj = lane & (C - 1)
    left = lane < C
    eye = (ii == jj).astype(f32)
    incl = (left & (ii >= jj)) | (~left & (ii <= jj))
    strict = incl & (ii != jj)
    level_masks = []
    sh = 0
    while (1 << sh) < C:
        level_masks.append(((ii >> (sh + 1)) == (jj >> (sh + 1))) & ((ii >> sh) != (jj >> sh)))
        sh += 1
    zero_rhs = jnp.zeros((C, 2 * GDN_DK), bf16)

    def prep(it, carry):
        units = []
        for cc in range(PREP_CHUNKS):
            c = it * PREP_CHUNKS + cc
            rows = pl.ds(pl.multiple_of(c * C, C), C)
            gc_tile = gcol_ref[rows, :]
            gr_tile = grow_ref[c]
            for hl in range(HEAD_GROUP):
                ls = slice(hl * GDN_DK, (hl + 1) * GDN_DK)
                kc = k_ref[rows, ls]
                qc = q_ref[rows, ls]
                k16 = kc.astype(bf16)
                kk16 = jnp.concatenate([k16, k16], axis=0)
                l0 = 8 * hl
                units.append(dict(
                    c=c, rows=rows, hl=hl, kc=kc, qc=qc, vc=v_ref[rows, ls],
                    gram=_dot_nt(k16, kk16), qk=_dot_nt(qc.astype(bf16), kk16),
                    gcol=(gc_tile[:, l0:l0 + 1], gc_tile[:, l0 + 1:l0 + 2]),
                    bcol=(gc_tile[:, l0 + 2:l0 + 3], gc_tile[:, l0 + 3:l0 + 4]),
                    grow=jnp.where(left[0:1], gr_tile[l0:l0 + 1, :], gr_tile[l0 + 1:l0 + 2, :])))
        for un in units:
            diff = jnp.where(left, un["gcol"][0], un["gcol"][1]) - un["grow"]
            un["decay"] = jnp.where(incl, jnp.exp(jnp.where(incl, diff, 0.0)), 0.0)
            bpair = jnp.where(left, un["bcol"][0], un["bcol"][1])
            un["a"] = jnp.where(strict, un["gram"] * un["decay"] * bpair, 0.0)
        ts = _tri_inverse_pairs([un["a"] for un in units], eye, level_masks, left)
        for un, t in zip(units, ts):
            un["eg"] = [jnp.exp(g) for g in un["gcol"]]
            halves = []
            for d in range(2):
                b = un["bcol"][d]
                halves.append(jnp.concatenate([(un["vc"] * b).astype(bf16),
                                               (un["kc"] * (b * un["eg"][d])).astype(bf16)], axis=1))
            rhs = jnp.concatenate([jnp.concatenate([halves[0], zero_rhs], axis=1),
                                   jnp.concatenate([zero_rhs, halves[1]], axis=1)], axis=0)
            un["uw"] = _dot(t.astype(bf16), rhs)
        for un in units:
            c = un["c"]
            rows = un["rows"]
            intra = un["qk"] * un["decay"]
            for d in range(2):
                idx = un["hl"] * 2 + d
                gcol = un["gcol"][d]
                glast = gcol[C - 1:C] if d == 0 else gcol[0:1]
                base = 2 * d * GDN_DK
                u_s[idx, rows, :] = un["uw"][:, base:base + GDN_DK]
                wq_s[idx, c, 0:C, :] = un["uw"][:, base + GDN_DK:base + 2 * GDN_DK].astype(bf16)
                wq_s[idx, c, C:2 * C, :] = (un["qc"] * un["eg"][d]).astype(bf16)
                kt_s[idx, rows, :] = (un["kc"] * jnp.exp(glast - gcol)).astype(bf16)
                in_s[idx, c] = jnp.where(left if d == 0 else ~left, intra, 0.0).astype(bf16)
                eg_s[idx, c] = jnp.broadcast_to(jnp.exp(glast), (8, GDN_DK))
        return carry

    lax.fori_loop(0, TM // C // PREP_CHUNKS, prep, 0)

    lanes = []
    for si in range(n_seq):
        for hl in range(HEAD_GROUP):
            for d in range(2):
                lanes.append((si, hl, d))
    for li, (si, hl, d) in enumerate(lanes):
        st_s[li] = s0_ref[si, 0, d, hl] if has_s0 else jnp.zeros((GDN_DK, GDN_DK), f32)

    def scan(t, carry):
        work = []
        for li, (si, hl, d) in enumerate(lanes):
            c = si * chunks_per_seq + (t if d == 0 else chunks_per_seq - 1 - t)
            rows = pl.ds(pl.multiple_of(c * C, C), C)
            work.append(dict(li=li, idx=hl * 2 + d, hl=hl, d=d, c=c, rows=rows, s=st_s[li]))
        for w in work:
            w["wsqs"] = _dot(wq_s[w["idx"], w["c"]], w["s"].astype(bf16))
        for w in work:
            v16 = (u_s[w["idx"], w["rows"], :] - w["wsqs"][:C]).astype(bf16)
            w["o"] = w["wsqs"][C:] + _dot(in_s[w["idx"], w["c"]], jnp.concatenate([v16, v16], axis=0))
            w["ds"] = _dot_tn(kt_s[w["idx"], w["rows"], :], v16)
        for w in work:
            ls = slice(w["hl"] * GDN_DK, (w["hl"] + 1) * GDN_DK)
            oo_s[w["d"], w["rows"], ls] = w["o"]
            st_s[w["li"]] = w["s"] * eg_s[w["idx"], w["c"]][0:1, :] + w["ds"]
        return carry

    lax.fori_loop(0, chunks_per_seq, scan, 0)

    ng = ng_ref[...]
    for hl in range(HEAD_GROUP):
        ls = slice(hl * GDN_DK, (hl + 1) * GDN_DK)
        o = oo_s[0, :, ls] + oo_s[1, :, ls]
        o_ref[:, ls] = (_rms(o) * ng * _silu(z_ref[:, ls])).astype(bf16)
    if emit_state:
        for li, (si, hl, d) in enumerate(lanes):
            sfin_ref[si, d, hl] = st_s[li]


def _gdn_core(qkvz, gcol, grow, norm_g_row, *, seq, tile0, n_tiles, s0=None, s0_layer=0, emit_state=False):
    hw = HEAD_GROUP * GDN_DK
    n_seq = TM // seq
    n_chunks = TM // GDN_CHUNK
    n_ch = HEAD_GROUP * 2

    def col_spec(base):
        return pl.BlockSpec((TM, hw), lambda s, hg: (tile0 + s, base * N_HEAD_GROUPS + hg))

    in_specs = [col_spec(0), col_spec(1), col_spec(2), col_spec(3),
                pl.BlockSpec((TM, GATE_LANES), lambda s, hg: (tile0 + s, hg)),
                pl.BlockSpec((n_chunks, GATE_LANES, 2 * GDN_CHUNK), lambda s, hg: (tile0 + s, hg, 0)),
                pl.BlockSpec((1, GDN_DK), lambda s, hg: (0, 0))]
    args = [qkvz, qkvz, qkvz, qkvz, gcol, grow, norm_g_row]
    if s0 is not None:
        in_specs.append(pl.BlockSpec((n_seq, 1, 2, HEAD_GROUP, GDN_DK, GDN_DK),
                                     lambda s, hg: (s, s0_layer, 0, hg, 0, 0)))
        args.append(s0)
    out_specs = [pl.BlockSpec((TM, hw), lambda s, hg: (s, hg))]
    out_shape = [jax.ShapeDtypeStruct((n_tiles * TM, D_MODEL), bf16)]
    if emit_state:
        out_specs.append(pl.BlockSpec((n_seq, 2, HEAD_GROUP, GDN_DK, GDN_DK), lambda s, hg: (s, 0, hg, 0, 0)))
        out_shape.append(jax.ShapeDtypeStruct((n_tiles * n_seq, 2, GDN_HEADS, GDN_DK, GDN_DK), f32))

    return pl.pallas_call(
        functools.partial(_gdn_core_kernel, seq=seq, has_s0=s0 is not None, emit_state=emit_state),
        grid=(n_tiles, N_HEAD_GROUPS),
        in_specs=in_specs,
        out_specs=out_specs,
        out_shape=out_shape,
        scratch_shapes=[
            pltpu.VMEM((n_ch, TM, GDN_DK), f32),
            pltpu.VMEM((n_ch, n_chunks, 2 * GDN_CHUNK, GDN_DK), bf16),
            pltpu.VMEM((n_ch, TM, GDN_DK), bf16),
            pltpu.VMEM((n_ch, n_chunks, GDN_CHUNK, 2 * GDN_CHUNK), bf16),
            pltpu.VMEM((n_ch, n_chunks, 8, GDN_DK), f32),
            pltpu.VMEM((n_seq * n_ch, GDN_DK, GDN_DK), f32),
            pltpu.VMEM((2, TM, hw), f32),
        ],
        compiler_params=_params(("arbitrary", "arbitrary")),
        name="gdn_core_%d" % seq,
    )(*args)


def _ctx_attn_kernel(q_ref, k_ref, v_ref, o_ref, ko_ref, vo_ref):
    k = k_ref[...]
    v = v_ref[...]
    ko_ref[...] = k
    vo_ref[...] = v
    heads = []
    for h in range(NA_HEADS):
        hs = slice(h * NA_DH, (h + 1) * NA_DH)
        heads.append(dict(s=_dot_nt(q_ref[:, hs] * (NA_DH ** -0.5), k[:, hs].astype(bf16)),
                          v16=v[:, hs].astype(bf16)))
    for hd in heads:
        e = jnp.exp(hd["s"] - jnp.max(hd["s"], axis=-1, keepdims=True))
        hd["p16"] = (e / jnp.sum(e, axis=-1, keepdims=True)).astype(bf16)
    outs = [_dot(hd["p16"], hd["v16"]) for hd in heads]
    for hp in range(NA_HEADS // 2):
        o_ref[:, 2 * hp * NA_DH:(2 * hp + 2) * NA_DH] = jnp.concatenate(
            outs[2 * hp:2 * hp + 2], axis=1).astype(bf16)


def _ctx_attn(qkv16, kv32):
    cache_shape = jax.ShapeDtypeStruct((N_PROMPT, D_MODEL), f32)
    row_spec = pl.BlockSpec((SEQ, D_MODEL), lambda b: (b, 0))
    return pl.pallas_call(
        _ctx_attn_kernel,
        grid=(BATCH,),
        in_specs=[row_spec, row_spec, pl.BlockSpec((SEQ, D_MODEL), lambda b: (b, 1))],
        out_specs=[row_spec, row_spec, row_spec],
        out_shape=[jax.ShapeDtypeStruct((N_PROMPT, D_MODEL), bf16), cache_shape, cache_shape],
        compiler_params=_params(("arbitrary",)),
        name="ctx_attn",
    )(qkv16, kv32, kv32)


def _window_row_start(r):
    return min(max(r - NA_KR // 2, 0), GRID_ROWS - NA_KR)


def _na_attn_kernel(q_ref, k_ref, v_ref, ck_ref, cv_ref, bias_ref, o_ref):
    kk = [k_ref[:, hh * NA_DH:(hh + 1) * NA_DH] for hh in range(2)]
    vv = [v_ref[:, hh * NA_DH:(hh + 1) * NA_DH] for hh in range(2)]
    ck = [ck_ref[0, 0, :, hh * NA_DH:(hh + 1) * NA_DH].astype(bf16) for hh in range(2)]
    cv = [cv_ref[0, 0, :, hh * NA_DH:(hh + 1) * NA_DH].astype(bf16) for hh in range(2)]
    for r0 in range(0, GRID_ROWS, NA_ROW_GROUP):
        units = []
        for r in range(r0, r0 + NA_ROW_GROUP):
            rs = _window_row_start(r)
            for hh in range(2):
                q = q_ref[r * GRID_W:(r + 1) * GRID_W, hh * NA_DH:(hh + 1) * NA_DH] * (NA_DH ** -0.5)
                units.append(dict(r=r, hh=hh, d0=rs - r + NA_KR - 1, q=q,
                                  win=slice(rs * GRID_W, (rs + NA_KR) * GRID_W)))
        for un in units:
            hh = un["hh"]
            d0 = un["d0"]
            lane0 = (d0 - d0 % 2) * GRID_W
            un["s_loc"] = (_dot_nt(un["q"], kk[hh][un["win"]])
                           + bias_ref[0, hh, d0 % 2, :, lane0:lane0 + NA_KR * GRID_W])
            un["s_ctx"] = _dot_nt(un["q"], ck[hh])
        for un in units:
            m = jnp.maximum(jnp.max(un["s_loc"], axis=-1, keepdims=True),
                            jnp.max(un["s_ctx"], axis=-1, keepdims=True))
            e_loc = jnp.exp(un["s_loc"] - m)
            e_ctx = jnp.exp(un["s_ctx"] - m)
            un["den"] = jnp.sum(e_loc, axis=-1, keepdims=True) + jnp.sum(e_ctx, axis=-1, keepdims=True)
            un["e_loc"] = e_loc.astype(bf16)
            un["e_ctx"] = e_ctx.astype(bf16)
        for un in units:
            hh = un["hh"]
            un["o"] = (_dot(un["e_loc"], vv[hh][un["win"]]) + _dot(un["e_ctx"], cv[hh])) / un["den"]
        for k in range(NA_ROW_GROUP):
            r = r0 + k
            o_ref[r * GRID_W:(r + 1) * GRID_W, :] = jnp.concatenate(
                [units[2 * k]["o"], units[2 * k + 1]["o"]], axis=1).astype(bf16)


def _na_attn(qkv16, cache_k, cache_v, bias, layer_slot):
    pw = 2 * NA_DH
    n_pairs = NA_HEADS // 2
    rb0 = N_PROMPT // DEC_SEQ
    return pl.pallas_call(
        _na_attn_kernel,
        grid=(n_pairs, DEC_BATCH),
        in_specs=[
            pl.BlockSpec((DEC_SEQ, pw), lambda hp, b: (rb0 + b, hp)),
            pl.BlockSpec((DEC_SEQ, pw), lambda hp, b: (rb0 + b, n_pairs + hp)),
            pl.BlockSpec((DEC_SEQ, pw), lambda hp, b: (rb0 + b, 2 * n_pairs + hp)),
            pl.BlockSpec((1, 1, PAST_LEN, pw), lambda hp, b: (b, layer_slot, 0, hp)),
            pl.BlockSpec((1, 1, PAST_LEN, pw), lambda hp, b: (b, layer_slot, 0, hp)),
            pl.BlockSpec((1, 2, 2, GRID_W, 2 * NA_KR * GRID_W), lambda hp, b: (layer_slot, hp, 0, 0, 0)),
        ],
        out_specs=pl.BlockSpec((DEC_SEQ, pw), lambda hp, b: (b, hp)),
        out_shape=jax.ShapeDtypeStruct((N_SAMPLE, D_MODEL), bf16),
        compiler_params=_params(("arbitrary", "arbitrary")),
        name="na_attn",
    )(qkv16, qkv16, qkv16, cache_k, cache_v, bias)


def _na_bias_windows(rel_bias):
    n_off = 2 * NA_KR - 1
    col = np.arange(GRID_W)
    col_start = np.clip(col - NA_KC // 2, 0, GRID_W - NA_KC)
    col_mask = (col[None, :] >= col_start[:, None]) & (col[None, :] < col_start[:, None] + NA_KC)
    col_idx = col[None, :] - col[:, None] + NA_KC - 1
    onehot = (col_idx[:, :, None] == np.arange(2 * NA_KC - 1)[None, None, :]) & col_mask[:, :, None]
    g = jnp.einsum("lhji,cki->lhcjk", rel_bias, jnp.asarray(onehot, f32), precision=lax.Precision.HIGHEST)
    g = jnp.where(col_mask[None, None, :, None, :], g, NEG_INF)
    g = g.reshape(g.shape[0], NA_HEADS, GRID_W, n_off * GRID_W)
    pad = (n_off + 1) * GRID_W - g.shape[-1]
    even = jnp.pad(g, ((0, 0), (0, 0), (0, 0), (0, pad)))
    odd = jnp.pad(g[..., GRID_W:], ((0, 0), (0, 0), (0, 0), (0, pad + GRID_W)))
    return jnp.stack([even, odd], axis=2)


def _post_kernel(x_ref, xp_ref, xn_ref, op_ref, opp_ref, opn_ref, os_ref, osp_ref, osn_ref, mod_ref, g_ref,
                 wout_ref, wup_ref, cw_ref, cb_ref, wdn_ref, fg_ref, *rest, final):
    if final:
        yp_ref, ys_ref, xe_s, oe_s, he_s, acc_s = rest
    else:
        out_ref, xe_s, oe_s, he_s, acc_s = rest
    i = pl.program_id(0)
    prompt_tiles = N_PROMPT // TP
    ext = TP + 2 * HALO
    xe_s[0:HALO] = xp_ref[...]
    xe_s[HALO:HALO + TP] = x_ref[...]
    xe_s[HALO + TP:ext] = xn_ref[...]

    @pl.when(i < prompt_tiles)
    def _():
        oe_s[0:HALO] = opp_ref[...]
        oe_s[HALO:HALO + TP] = op_ref[...]
        oe_s[HALO + TP:ext] = opn_ref[...]

    @pl.when(i >= prompt_tiles)
    def _():
        oe_s[0:HALO] = osp_ref[...]
        oe_s[HALO:HALO + TP] = os_ref[...]
        oe_s[HALO + TP:ext] = osn_ref[...]

    m = mod_ref[0, 0]
    x1 = xe_s[...] + m[2:3] * _dot(oe_s[...], wout_ref[0])
    xe_s[...] = x1
    he_s[...] = _norm_mod(x1, g_ref[0], m[3:4], m[4:5]).astype(bf16)

    seq = _seq_len_of_tile(i, TP)

    @pl.when(((i * TP) & (seq - 1)) == 0)
    def _():
        he_s[0:HALO] = jnp.zeros((HALO, D_MODEL), bf16)

    @pl.when((((i + 1) * TP) & (seq - 1)) == 0)
    def _():
        he_s[HALO + TP:ext] = jnp.zeros((HALO, D_MODEL), bf16)

    slab = slice(SEQ - HALO, SEQ + HALO)
    pos = (lax.broadcasted_iota(jnp.int32, (TP, 1), 0)[slab] + i * TP) & (seq - 1)
    first = pos == 0
    last = pos == seq - 1

    def conv(up, c0, width):
        cw = cw_ref[0, :, c0:c0 + width]
        cb = cb_ref[0, :, c0:c0 + width]
        prev = pltpu.roll(up, 1, axis=0)[HALO:HALO + TP]
        mid = up[HALO:HALO + TP]
        nxt = pltpu.roll(up, ext - 1, axis=0)[HALO:HALO + TP]
        u = cw[0:1] * prev + cw[1:2] * mid + cw[2:3] * nxt + cb
        u_slab = (cw[0:1] * jnp.where(first, 0.0, prev[slab]) + cw[1:2] * mid[slab]
                  + cw[2:3] * jnp.where(last, 0.0, nxt[slab]) + cb)
        return jnp.concatenate([u[:slab.start], u_slab, u[slab.stop:]], axis=0)

    c0 = 0
    while c0 < D_FF:
        width = min(FF_CHUNK, D_FF - c0)
        he = he_s[...]
        val = conv(_dot(he, wup_ref[0, :, c0:c0 + width]), c0, width)
        gate = conv(_dot(he, wup_ref[0, :, D_FF + c0:D_FF + c0 + width]), D_FF + c0, width)
        down = _dot((_silu(gate) * val).astype(bf16), wdn_ref[0, c0:c0 + width, :])
        if c0 == 0:
            acc_s[...] = down
        else:
            acc_s[...] += down
        c0 += width

    y = xe_s[HALO:HALO + TP] + m[5:6] * acc_s[...]
    if final:
        y = _rms(y) * fg_ref[...]

        @pl.when(i < prompt_tiles)
        def _():
            yp_ref[...] = y

        @pl.when(i >= prompt_tiles)
        def _():
            ys_ref[...] = y
    else:
        out_ref[...] = y


def _post(x, og_p, og_s, mod, norm_g, w_out, w_up, conv_w, conv_b, w_down, final_g, layer, final):
    n_tiles = N_TOK // TP
    prompt_tiles = N_PROMPT // TP
    hb = TP // HALO
    mixer_slot = layer // 2

    def halo_maps(tile_of, n_rows):
        n_hb = n_rows // HALO
        last_tile = n_rows // TP - 1

        def main(i):
            return (jnp.clip(tile_of(i), 0, last_tile), 0)

        def prev(i):
            return (jnp.clip(tile_of(i) * hb - 1, 0, n_hb - 1), 0)

        def nxt(i):
            return (jnp.clip((tile_of(i) + 1) * hb, 0, n_hb - 1), 0)

        return main, prev, nxt

    x_maps = halo_maps(lambda i: i, N_TOK)
    p_maps = halo_maps(lambda i: i, N_PROMPT)
    s_maps = halo_maps(lambda i: i - prompt_tiles, N_SAMPLE)

    def mod_map(i):
        tiles_per_req = DEC_SEQ // TP
        return (layer, jnp.maximum((i - (prompt_tiles - tiles_per_req)) // tiles_per_req, 0), 0, 0)

    def triple(maps):
        return [pl.BlockSpec((TP, D_MODEL), maps[0]), pl.BlockSpec((HALO, D_MODEL), maps[1]),
                pl.BlockSpec((HALO, D_MODEL), maps[2])]

    resident = dict(pipeline_mode=pl.Buffered(1))
    ext = TP + 2 * HALO
    if final:
        out_specs = [pl.BlockSpec((TP, D_MODEL), p_maps[0]), pl.BlockSpec((TP, D_MODEL), s_maps[0])]
        out_shape = [jax.ShapeDtypeStruct((N_PROMPT, D_MODEL), f32), jax.ShapeDtypeStruct((N_SAMPLE, D_MODEL), f32)]
    else:
        out_specs = pl.BlockSpec((TP, D_MODEL), lambda i: (i, 0))
        out_shape = jax.ShapeDtypeStruct((N_TOK, D_MODEL), f32)
    return pl.pallas_call(
        functools.partial(_post_kernel, final=final),
        grid=(n_tiles,),
        in_specs=triple(x_maps) + triple(p_maps) + triple(s_maps) + [
            pl.BlockSpec((1, 1, N_MOD, D_MODEL), mod_map),
            pl.BlockSpec((1, 1, D_MODEL), lambda i: (layer, 0, 0)),
            pl.BlockSpec((1, D_MODEL, D_MODEL), lambda i: (mixer_slot, 0, 0), **resident),
            pl.BlockSpec((1, D_MODEL, 2 * D_FF), lambda i: (layer, 0, 0), **resident),
            pl.BlockSpec((1, 3, 2 * D_FF), lambda i: (layer, 0, 0), **resident),
            pl.BlockSpec((1, 1, 2 * D_FF), lambda i: (layer, 0, 0), **resident),
            pl.BlockSpec((1, D_FF, D_MODEL), lambda i: (layer, 0, 0), **resident),
            pl.BlockSpec((1, D_MODEL), lambda i: (0, 0)),
        ],
        out_specs=out_specs,
        out_shape=out_shape,
        scratch_shapes=[
            pltpu.VMEM((ext, D_MODEL), f32),
            pltpu.VMEM((ext, D_MODEL), bf16),
            pltpu.VMEM((ext, D_MODEL), bf16),
            pltpu.VMEM((TP, D_MODEL), f32),
        ],
        compiler_params=_params(("arbitrary",)),
        name="post",
    )(x, x, x, og_p, og_p, og_p, og_s, og_s, og_s, mod, norm_g, w_out, w_up, conv_w, conv_b, w_down, final_g)


def _gate_lanes(t):
    lead = t.shape[:-2]
    t = t.reshape(lead + (4, N_HEAD_GROUPS, HEAD_GROUP))
    t = jnp.moveaxis(t, -3, -1)
    t = jnp.pad(t, [(0, 0)] * (len(lead) + 2) + [(0, 4)])
    t = t.reshape(lead + (N_HEAD_GROUPS, 8 * HEAD_GROUP))
    t = jnp.pad(t, [(0, 0)] * (len(lead) + 1) + [(0, GATE_LANES - 8 * HEAD_GROUP)])
    return t.reshape(lead + (N_HEAD_GROUPS * GATE_LANES,))


def _gdn_gate_layout(w_in, a_log, dt_bias):
    w_ab = _gate_lanes(w_in[:, 4 * D_MODEL:].reshape(D_MODEL, 4, GDN_HEADS)).astype(bf16)
    zeros = jnp.zeros_like(a_log)
    alog_row = _gate_lanes(jnp.concatenate([a_log, zeros], axis=0))[None, :]
    dtb_row = _gate_lanes(jnp.concatenate([dt_bias, zeros], axis=0))[None, :]
    return w_ab, alog_row.astype(f32), dtb_row.astype(f32)


def kernel(x_prompt, x_sample, state_gdn, cache_k, cache_v, c, c_ctx, w_ada, b_ada, norm1_g, norm2_g,
           gdn_w_in, gdn_conv_w, gdn_a_log, gdn_dt_bias, gdn_norm_g, gdn_w_out,
           na_w_qkv, na_rel_bias, na_w_out, ffn_w_up, ffn_conv_w, ffn_conv_b, ffn_w_down, final_g):
    x = jnp.concatenate([x_prompt.reshape(N_PROMPT, D_MODEL), x_sample.reshape(N_SAMPLE, D_MODEL)], axis=0)
    cvec = jnp.concatenate([c_ctx[None, :], c, jnp.zeros((MOD_ROWS - 1 - DEC_BATCH, D_MODEL), f32)], axis=0)
    mod = _modulation(cvec, w_ada, b_ada).reshape(DEPTH, MOD_ROWS, N_MOD, D_MODEL)
    n1 = norm1_g.reshape(DEPTH, 1, D_MODEL)
    n2 = norm2_g.reshape(DEPTH, 1, D_MODEL)
    fg = final_g.reshape(1, D_MODEL)
    ck = cache_k.reshape(DEC_BATCH, 2, PAST_LEN, D_MODEL)
    cv = cache_v.reshape(DEC_BATCH, 2, PAST_LEN, D_MODEL)
    bias = _na_bias_windows(na_rel_bias)
    prompt_tiles = N_PROMPT // TM
    gdn_w_in16 = gdn_w_in.astype(bf16)
    na_w_qkv16 = na_w_qkv.astype(bf16)
    w_out16 = (gdn_w_out.astype(bf16), na_w_out.astype(bf16))
    ffn_w_up16 = ffn_w_up.astype(bf16)
    ffn_w_down16 = ffn_w_down.astype(bf16)
    ffn_conv_b3 = ffn_conv_b.reshape(DEPTH, 1, 2 * D_FF)

    states, new_ks, new_vs = [], [], []
    for l in range(DEPTH):
        j = l // 2
        if l % 2 == 0:
            w_ab, alog_row, dtb_row = _gdn_gate_layout(gdn_w_in[j], gdn_a_log[j], gdn_dt_bias[j])
            qkvz, gcol, grow = _gdn_in(x, mod, n1, gdn_w_in16, w_ab, gdn_conv_w, alog_row, dtb_row, l)
            ng = gdn_norm_g[j].reshape(1, GDN_DK)
            og_p, st = _gdn_core(qkvz, gcol, grow, ng, seq=SEQ, tile0=0, n_tiles=prompt_tiles, emit_state=True)
            og_s, = _gdn_core(qkvz, gcol, grow, ng, seq=DEC_SEQ, tile0=prompt_tiles, n_tiles=DEC_BATCH,
                              s0=state_gdn, s0_layer=j)
            states.append(st)
        else:
            qkv16, kv32 = _na_in(x, mod, n1, na_w_qkv16, l)
            og_p, nk, nv = _ctx_attn(qkv16, kv32)
            og_s = _na_attn(qkv16, ck, cv, bias, j)
            new_ks.append(nk.reshape(BATCH, SEQ, NA_HEADS, NA_DH))
            new_vs.append(nv.reshape(BATCH, SEQ, NA_HEADS, NA_DH))
        x = _post(x, og_p, og_s, mod, n2, w_out16[l % 2], ffn_w_up16, ffn_conv_w, ffn_conv_b3, ffn_w_down16,
                  fg, l, l == DEPTH - 1)

    y_prompt = x[0].reshape(BATCH, SEQ, D_MODEL)
    y_sample = x[1].reshape(DEC_BATCH, DEC_SEQ, D_MODEL)
    return (y_prompt, y_sample, jnp.stack(states, axis=1), jnp.stack(new_ks, axis=1), jnp.stack(new_vs, axis=1))
```

```python
import functools

import jax
import jax.numpy as jnp
import numpy as np
from jax import lax
from jax.experimental import pallas as pl
from jax.experimental.pallas import tpu as pltpu

f32 = jnp.float32
bf16 = jnp.bfloat16

D_MODEL = 1024
BATCH = 16
SEQ = 256
DEPTH = 4
DEC_BATCH = 8
DEC_SEQ = 1024
PAST_LEN = 256
GRID_W = 64
GRID_ROWS = DEC_SEQ // GRID_W
GDN_HEADS = 8
GDN_DK = 128
GDN_CHUNK = 64
NA_HEADS = 16
NA_DH = 64
NA_KR = 8
NA_KC = 16
D_FF = 2816
N_MOD = 6
EPS = 1e-6
NEG_INF = -1e30

N_PROMPT = BATCH * SEQ
N_SAMPLE = DEC_BATCH * DEC_SEQ
N_TOK = N_PROMPT + N_SAMPLE

TM = 1024
TP = 512
HALO = 16
FF_CHUNK = 512
HEAD_GROUP = 4
N_HEAD_GROUPS = GDN_HEADS // HEAD_GROUP
GATE_LANES = 128
PREP_CHUNKS = 4
NA_ROW_GROUP = 4
MOD_ROWS = 16
MOD_TN = 1536

VMEM_LIMIT = 56 * 1024 * 1024


def _silu(x):
    return x * jax.nn.sigmoid(x)


def _dot(a, b):
    return jnp.dot(a, b, preferred_element_type=f32)


def _dot_nt(a, b):
    return lax.dot_general(a, b, (((1,), (1,)), ((), ())), preferred_element_type=f32)


def _dot_tn(a, b):
    return lax.dot_general(a, b, (((0,), (0,)), ((), ())), preferred_element_type=f32)


def _rms(x):
    return x * lax.rsqrt(jnp.mean(x * x, axis=-1, keepdims=True) + EPS)


def _params(sem, vmem=VMEM_LIMIT):
    return pltpu.CompilerParams(dimension_semantics=sem, vmem_limit_bytes=vmem)


def _mod_kernel(cv_ref, w_ref, b_ref, o_ref):
    s = _silu(cv_ref[...]).astype(bf16)
    o_ref[0] = _dot(s, w_ref[0].astype(bf16)) + b_ref[0]


def _modulation(cvec, w_ada, b_ada):
    n_tiles = (N_MOD * D_MODEL) // MOD_TN
    return pl.pallas_call(
        _mod_kernel,
        grid=(DEPTH, n_tiles),
        in_specs=[
            pl.BlockSpec((MOD_ROWS, D_MODEL), lambda l, n: (0, 0)),
            pl.BlockSpec((1, D_MODEL, MOD_TN), lambda l, n: (l, 0, n)),
            pl.BlockSpec((1, 1, MOD_TN), lambda l, n: (l, 0, n)),
        ],
        out_specs=pl.BlockSpec((1, MOD_ROWS, MOD_TN), lambda l, n: (l, 0, n)),
        out_shape=jax.ShapeDtypeStruct((DEPTH, MOD_ROWS, N_MOD * D_MODEL), f32),
        compiler_params=_params(("arbitrary", "arbitrary")),
        name="modulation",
    )(cvec, w_ada, b_ada.reshape(DEPTH, 1, N_MOD * D_MODEL))


def _mod_row_tm(i):
    return jnp.maximum(i - (N_PROMPT // TM - 1), 0)


def _norm_mod(x, g, shift, scale):
    return (_rms(x) * g) * (1.0 + scale) + shift


def _na_in_kernel(x_ref, mod_ref, g_ref, w_ref, o16_ref, kv_ref, h_ref):
    i = pl.program_id(0)
    n = pl.program_id(1)

    @pl.when(n == 0)
    def _():
        m = mod_ref[0, 0]
        h_ref[...] = _norm_mod(x_ref[...], g_ref[0], m[0:1], m[1:2]).astype(bf16)

    p = _dot(h_ref[...], w_ref[0])
    o16_ref[...] = p.astype(bf16)

    @pl.when((i < N_PROMPT // TM) & (n > 0))
    def _():
        kv_ref[...] = p


def _na_in(x, mod, norm_g, w, layer):
    n_out = w.shape[2] // D_MODEL
    mixer_slot = layer // 2
    prompt_tiles = N_PROMPT // TM

    def kv_map(i, n):
        return (jnp.minimum(i, prompt_tiles - 1), jnp.where(i < prompt_tiles, jnp.maximum(n - 1, 0), 1))

    return pl.pallas_call(
        _na_in_kernel,
        grid=(N_TOK // TM, n_out),
        in_specs=[
            pl.BlockSpec((TM, D_MODEL), lambda i, n: (i, 0)),
            pl.BlockSpec((1, 1, N_MOD, D_MODEL), lambda i, n: (layer, _mod_row_tm(i), 0, 0)),
            pl.BlockSpec((1, 1, D_MODEL), lambda i, n: (layer, 0, 0)),
            pl.BlockSpec((1, D_MODEL, D_MODEL), lambda i, n: (mixer_slot, 0, n)),
        ],
        out_specs=[pl.BlockSpec((TM, D_MODEL), lambda i, n: (i, n)),
                   pl.BlockSpec((TM, D_MODEL), kv_map)],
        out_shape=[jax.ShapeDtypeStruct((N_TOK, w.shape[2]), bf16),
                   jax.ShapeDtypeStruct((N_PROMPT, 2 * D_MODEL), f32)],
        scratch_shapes=[pltpu.VMEM((TM, D_MODEL), bf16)],
        compiler_params=_params(("arbitrary", "arbitrary")),
        name="na_in",
    )(x, mod, norm_g, w)


def _seq_len_of_tile(i, tile):
    return jnp.where(i < N_PROMPT // tile, SEQ, DEC_SEQ)


def _gdn_in_kernel(x_ref, mod_ref, g_ref, w_ref, wab_ref, cw_ref, alog_ref, dtb_ref,
                   o_ref, gcol_ref, grow_ref, h_ref):
    i = pl.program_id(0)
    n = pl.program_id(1)

    @pl.when(n == 0)
    def _():
        m = mod_ref[0, 0]
        h_ref[...] = _norm_mod(x_ref[...], g_ref[0], m[0:1], m[1:2]).astype(bf16)

    def conv_silu():
        p = _dot(h_ref[...], w_ref[0])
        seq = _seq_len_of_tile(i, TM)
        pos = lax.broadcasted_iota(jnp.int32, (TM, 1), 0) & (seq - 1)
        prev = jnp.where(pos == 0, 0.0, pltpu.roll(p, 1, axis=0))
        nxt = jnp.where(pos == seq - 1, 0.0, pltpu.roll(p, TM - 1, axis=0))
        cw = cw_ref[0]
        return _silu(cw[0:1] * prev + cw[1:2] * p + cw[2:3] * nxt)

    @pl.when(n < 2)
    def _():
        s = conv_silu()
        qscale = jnp.where(n == 0, GDN_DK ** -0.5, 1.0).astype(f32)
        for h in range(GDN_HEADS):
            sh = s[:, h * GDN_DK:(h + 1) * GDN_DK]
            nh = sh * lax.rsqrt(jnp.sum(sh * sh, axis=-1, keepdims=True) + EPS)
            o_ref[:, h * GDN_DK:(h + 1) * GDN_DK] = nh * qscale

    @pl.when(n == 2)
    def _():
        o_ref[...] = conv_silu()

    @pl.when(n == 3)
    def _():
        o_ref[...] = _dot(h_ref[...], w_ref[0])

    @pl.when(n == 4)
    def _():
        ab = _dot(h_ref[...], wab_ref[...])
        width = ab.shape[1]
        kind = lax.broadcasted_iota(jnp.int32, (1, width), 1) & 7
        xs = ab + dtb_ref[...]
        softplus = jnp.maximum(xs, 0.0) + jnp.log1p(jnp.exp(-jnp.abs(xs)))
        g = -jnp.exp(alog_ref[...]) * softplus
        pos = lax.broadcasted_iota(jnp.int32, (TM, 1), 0) & (GDN_CHUNK - 1)
        cf = g
        cb = g
        s = 1
        while s < GDN_CHUNK:
            cf = cf + jnp.where(pos >= s, pltpu.roll(cf, s, axis=0), 0.0)
            cb = cb + jnp.where(pos < GDN_CHUNK - s, pltpu.roll(cb, TM - s, axis=0), 0.0)
            s *= 2
        gates = jnp.where(kind == 0, cf, jnp.where(kind == 1, cb, jnp.where(kind < 4, jax.nn.sigmoid(ab), 0.0)))
        gcol_ref[...] = gates
        for c in range(TM // GDN_CHUNK):
            for hg in range(width // GATE_LANES):
                bt = gates[c * GDN_CHUNK:(c + 1) * GDN_CHUNK, hg * GATE_LANES:(hg + 1) * GATE_LANES].T
                grow_ref[c, hg * GATE_LANES:(hg + 1) * GATE_LANES, :] = jnp.concatenate([bt, bt], axis=1)


def _gdn_in(x, mod, norm_g, w_in, w_ab, conv_w, alog_row, dtb_row, layer):
    n_chunks = N_TOK // GDN_CHUNK
    gw = N_HEAD_GROUPS * GATE_LANES
    n_proj = 4
    mixer_slot = layer // 2
    return pl.pallas_call(
        _gdn_in_kernel,
        grid=(N_TOK // TM, n_proj + 1),
        in_specs=[
            pl.BlockSpec((TM, D_MODEL), lambda i, n: (i, 0)),
            pl.BlockSpec((1, 1, N_MOD, D_MODEL), lambda i, n: (layer, _mod_row_tm(i), 0, 0)),
            pl.BlockSpec((1, 1, D_MODEL), lambda i, n: (layer, 0, 0)),
            pl.BlockSpec((1, D_MODEL, D_MODEL), lambda i, n: (mixer_slot, 0, jnp.minimum(n, n_proj - 1))),
            pl.BlockSpec((D_MODEL, gw), lambda i, n: (0, 0)),
            pl.BlockSpec((1, 3, D_MODEL), lambda i, n: (mixer_slot, 0, jnp.minimum(n, 2))),
            pl.BlockSpec((1, gw), lambda i, n: (0, 0)),
            pl.BlockSpec((1, gw), lambda i, n: (0, 0)),
        ],
        out_specs=[
            pl.BlockSpec((TM, D_MODEL), lambda i, n: (i, jnp.minimum(n, n_proj - 1))),
            pl.BlockSpec((TM, gw), lambda i, n: (i, 0)),
            pl.BlockSpec((TM // GDN_CHUNK, gw, 2 * GDN_CHUNK), lambda i, n: (i, 0, 0)),
        ],
        out_shape=[
            jax.ShapeDtypeStruct((N_TOK, n_proj * D_MODEL), f32),
            jax.ShapeDtypeStruct((N_TOK, gw), f32),
            jax.ShapeDtypeStruct((n_chunks, gw, 2 * GDN_CHUNK), f32),
        ],
        scratch_shapes=[pltpu.VMEM((TM, D_MODEL), bf16)],
        compiler_params=_params(("arbitrary", "arbitrary")),
        name="gdn_in",
    )(x, mod, norm_g, w_in, w_ab, conv_w, alog_row, dtb_row)


def _block_diag(x, left):
    return jnp.concatenate([jnp.where(left, x, 0.0), jnp.where(left, 0.0, x)], axis=0).astype(bf16)


def _tri_inverse_pairs(a_list, eye, level_masks, left):
    ts = [eye - jnp.where(level_masks[0], a, 0.0) for a in a_list]
    for mask in level_masks[1:]:
        tls = [_dot(t.astype(bf16), _block_diag(jnp.where(mask, a, 0.0), left)) for t, a in zip(ts, a_list)]
        ts = [t - _dot(tl.astype(bf16), _block_diag(t, left)) for t, tl in zip(ts, tls)]
    return ts


def _gdn_core_kernel(*refs, seq, has_s0, emit_state):
    q_ref, k_ref, v_ref, z_ref, gcol_ref, grow_ref, ng_ref = refs[:7]
    pos = 7
    s0_ref = None
    if has_s0:
        s0_ref = refs[pos]
        pos += 1
    o_ref = refs[pos]
    pos += 1
    sfin_ref = None
    if emit_state:
        sfin_ref = refs[pos]
        pos += 1
    u_s, wq_s, kt_s, in_s, eg_s, st_s, oo_s = refs[pos:]

    C = GDN_CHUNK
    n_seq = TM // seq
    chunks_per_seq = seq // C
    ii = lax.broadcasted_iota(jnp.int32, (C, 2 * C), 0)
    lane = lax.broadcasted_iota(jnp.int32, (C, 2 * C), 1)
    jj = lane & (C - 1)
    left = lane < C
    eye = (ii == jj).astype(f32)
    incl = (left & (ii >= jj)) | (~left & (ii <= jj))
    strict = incl & (ii != jj)
    level_masks = []
    sh = 0
    while (1 << sh) < C:
        level_masks.append(((ii >> (sh + 1)) == (jj >> (sh + 1))) & ((ii >> sh) != (jj >> sh)))
        sh += 1
    zero_rhs = jnp.zeros((C, 2 * GDN_DK), bf16)

    def prep(it, carry):
        units = []
        for cc in range(PREP_CHUNKS):
            c = it * PREP_CHUNKS + cc
            rows = pl.ds(pl.multiple_of(c * C, C), C)
            gc_tile = gcol_ref[rows, :]
            gr_tile = grow_ref[c]
            for hl in range(HEAD_GROUP):
                ls = slice(hl * GDN_DK, (hl + 1) * GDN_DK)
                kc = k_ref[rows, ls]
                qc = q_ref[rows, ls]
                k16 = kc.astype(bf16)
                kk16 = jnp.concatenate([k16, k16], axis=0)
                l0 = 8 * hl
                units.append(dict(
                    c=c, rows=rows, hl=hl, kc=kc, qc=qc, vc=v_ref[rows, ls],
                    gram=_dot_nt(k16, kk16), qk=_dot_nt(qc.astype(bf16), kk16),
                    gcol=(gc_tile[:, l0:l0 + 1], gc_tile[:, l0 + 1:l0 + 2]),
                    bcol=(gc_tile[:, l0 + 2:l0 + 3], gc_tile[:, l0 + 3:l0 + 4]),
                    grow=jnp.where(left[0:1], gr_tile[l0:l0 + 1, :], gr_tile[l0 + 1:l0 + 2, :])))
        for un in units:
            diff = jnp.where(left, un["gcol"][0], un["gcol"][1]) - un["grow"]
            un["decay"] = jnp.where(incl, jnp.exp(jnp.where(incl, diff, 0.0)), 0.0)
            bpair = jnp.where(left, un["bcol"][0], un["bcol"][1])
            un["a"] = jnp.where(strict, un["gram"] * un["decay"] * bpair, 0.0)
        ts = _tri_inverse_pairs([un["a"] for un in units], eye, level_masks, left)
        for un, t in zip(units, ts):
            un["eg"] = [jnp.exp(g) for g in un["gcol"]]
            halves = []
            for d in range(2):
                b = un["bcol"][d]
                halves.append(jnp.concatenate([(un["vc"] * b).astype(bf16),
                                               (un["kc"] * (b * un["eg"][d])).astype(bf16)], axis=1))
            rhs = jnp.concatenate([jnp.concatenate([halves[0], zero_rhs], axis=1),
                                   jnp.concatenate([zero_rhs, halves[1]], axis=1)], axis=0)
            un["uw"] = _dot(t.astype(bf16), rhs)
        for un in units:
            c = un["c"]
            rows = un["rows"]
            intra = un["qk"] * un["decay"]
            for d in range(2):
                idx = un["hl"] * 2 + d
                gcol = un["gcol"][d]
                glast = gcol[C - 1:C] if d == 0 else gcol[0:1]
                base = 2 * d * GDN_DK
                u_s[idx, rows, :] = un["uw"][:, base:base + GDN_DK]
                wq_s[idx, c, 0:C, :] = un["uw"][:, base + GDN_DK:base + 2 * GDN_DK].astype(bf16)
                wq_s[idx, c, C:2 * C, :] = (un["qc"] * un["eg"][d]).astype(bf16)
                kt_s[idx, rows, :] = (un["kc"] * jnp.exp(glast - gcol)).astype(bf16)
                in_s[idx, c] = jnp.where(left if d == 0 else ~left, intra, 0.0).astype(bf16)
                eg_s[idx, c] = jnp.broadcast_to(jnp.exp(glast), (8, GDN_DK))
        return carry

    lax.fori_loop(0, TM // C // PREP_CHUNKS, prep, 0)

    lanes = []
    for si in range(n_seq):
        for hl in range(HEAD_GROUP):
            for d in range(2):
                lanes.append((si, hl, d))
    for li, (si, hl, d) in enumerate(lanes):
        st_s[li] = s0_ref[si, 0, d, hl] if has_s0 else jnp.zeros((GDN_DK, GDN_DK), f32)

    def scan(t, carry):
        work = []
        for li, (si, hl, d) in enumerate(lanes):
            c = si * chunks_per_seq + (t if d == 0 else chunks_per_seq - 1 - t)
            rows = pl.ds(pl.multiple_of(c * C, C), C)
            work.append(dict(li=li, idx=hl * 2 + d, hl=hl, d=d, c=c, rows=rows, s=st_s[li]))
        for w in work:
            w["wsqs"] = _dot(wq_s[w["idx"], w["c"]], w["s"].astype(bf16))
        for w in work:
            v16 = (u_s[w["idx"], w["rows"], :] - w["wsqs"][:C]).astype(bf16)
            w["o"] = w["wsqs"][C:] + _dot(in_s[w["idx"], w["c"]], jnp.concatenate([v16, v16], axis=0))
            w["ds"] = _dot_tn(kt_s[w["idx"], w["rows"], :], v16)
        for w in work:
            ls = slice(w["hl"] * GDN_DK, (w["hl"] + 1) * GDN_DK)
            oo_s[w["d"], w["rows"], ls] = w["o"]
            st_s[w["li"]] = w["s"] * eg_s[w["idx"], w["c"]][0:1, :] + w["ds"]
        return carry

    lax.fori_loop(0, chunks_per_seq, scan, 0)

    ng = ng_ref[...]
    for hl in range(HEAD_GROUP):
        ls = slice(hl * GDN_DK, (hl + 1) * GDN_DK)
        o = oo_s[0, :, ls] + oo_s[1, :, ls]
        o_ref[:, ls] = (_rms(o) * ng * _silu(z_ref[:, ls])).astype(bf16)
    if emit_state:
        for li, (si, hl, d) in enumerate(lanes):
            sfin_ref[si, d, hl] = st_s[li]


def _gdn_core(qkvz, gcol, grow, norm_g_row, *, seq, tile0, n_tiles, s0=None, s0_layer=0, emit_state=False):
    hw = HEAD_GROUP * GDN_DK
    n_seq = TM // seq
    n_chunks = TM // GDN_CHUNK
    n_ch = HEAD_GROUP * 2

    def col_spec(base):
        return pl.BlockSpec((TM, hw), lambda s, hg: (tile0 + s, base * N_HEAD_GROUPS + hg))

    in_specs = [col_spec(0), col_spec(1), col_spec(2), col_spec(3),
                pl.BlockSpec((TM, GATE_LANES), lambda s, hg: (tile0 + s, hg)),
                pl.BlockSpec((n_chunks, GATE_LANES, 2 * GDN_CHUNK), lambda s, hg: (tile0 + s, hg, 0)),
                pl.BlockSpec((1, GDN_DK), lambda s, hg: (0, 0))]
    args = [qkvz, qkvz, qkvz, qkvz, gcol, grow, norm_g_row]
    if s0 is not None:
        in_specs.append(pl.BlockSpec((n_seq, 1, 2, HEAD_GROUP, GDN_DK, GDN_DK),
                                     lambda s, hg: (s, s0_layer, 0, hg, 0, 0)))
        args.append(s0)
    out_specs = [pl.BlockSpec((TM, hw), lambda s, hg: (s, hg))]
    out_shape = [jax.ShapeDtypeStruct((n_tiles * TM, D_MODEL), bf16)]
    if emit_state:
        out_specs.append(pl.BlockSpec((n_seq, 2, HEAD_GROUP, GDN_DK, GDN_DK), lambda s, hg: (s, 0, hg, 0, 0)))
        out_shape.append(jax.ShapeDtypeStruct((n_tiles * n_seq, 2, GDN_HEADS, GDN_DK, GDN_DK), f32))

    return pl.pallas_call(
        functools.partial(_gdn_core_kernel, seq=seq, has_s0=s0 is not None, emit_state=emit_state),
        grid=(n_tiles, N_HEAD_GROUPS),
        in_specs=in_specs,
        out_specs=out_specs,
        out_shape=out_shape,
        scratch_shapes=[
            pltpu.VMEM((n_ch, TM, GDN_DK), f32),
            pltpu.VMEM((n_ch, n_chunks, 2 * GDN_CHUNK, GDN_DK), bf16),
            pltpu.VMEM((n_ch, TM, GDN_DK), bf16),
            pltpu.VMEM((n_ch, n_chunks, GDN_CHUNK, 2 * GDN_CHUNK), bf16),
            pltpu.VMEM((n_ch, n_chunks, 8, GDN_DK), f32),
            pltpu.VMEM((n_seq * n_ch, GDN_DK, GDN_DK), f32),
            pltpu.VMEM((2, TM, hw), f32),
        ],
        compiler_params=_params(("arbitrary", "arbitrary")),
        name="gdn_core_%d" % seq,
    )(*args)


def _ctx_attn_kernel(q_ref, k_ref, v_ref, o_ref, ko_ref, vo_ref):
    k = k_ref[...]
    v = v_ref[...]
    ko_ref[...] = k
    vo_ref[...] = v
    heads = []
    for h in range(NA_HEADS):
        hs = slice(h * NA_DH, (h + 1) * NA_DH)
        heads.append(dict(s=_dot_nt(q_ref[:, hs] * (NA_DH ** -0.5), k[:, hs].astype(bf16)),
                          v16=v[:, hs].astype(bf16)))
    for hd in heads:
        e = jnp.exp(hd["s"] - jnp.max(hd["s"], axis=-1, keepdims=True))
        hd["p16"] = (e / jnp.sum(e, axis=-1, keepdims=True)).astype(bf16)
    outs = [_dot(hd["p16"], hd["v16"]) for hd in heads]
    for hp in range(NA_HEADS // 2):
        o_ref[:, 2 * hp * NA_DH:(2 * hp + 2) * NA_DH] = jnp.concatenate(
            outs[2 * hp:2 * hp + 2], axis=1).astype(bf16)


def _ctx_attn(qkv16, kv32):
    cache_shape = jax.ShapeDtypeStruct((N_PROMPT, D_MODEL), f32)
    row_spec = pl.BlockSpec((SEQ, D_MODEL), lambda b: (b, 0))
    return pl.pallas_call(
        _ctx_attn_kernel,
        grid=(BATCH,),
        in_specs=[row_spec, row_spec, pl.BlockSpec((SEQ, D_MODEL), lambda b: (b, 1))],
        out_specs=[row_spec, row_spec, row_spec],
        out_shape=[jax.ShapeDtypeStruct((N_PROMPT, D_MODEL), bf16), cache_shape, cache_shape],
        compiler_params=_params(("arbitrary",)),
        name="ctx_attn",
    )(qkv16, kv32, kv32)


def _window_row_start(r):
    return min(max(r - NA_KR // 2, 0), GRID_ROWS - NA_KR)


def _na_attn_kernel(q_ref, k_ref, v_ref, ck_ref, cv_ref, bias_ref, o_ref):
    kk = [k_ref[:, hh * NA_DH:(hh + 1) * NA_DH] for hh in range(2)]
    vv = [v_ref[:, hh * NA_DH:(hh + 1) * NA_DH] for hh in range(2)]
    ck = [ck_ref[0, 0, :, hh * NA_DH:(hh + 1) * NA_DH].astype(bf16) for hh in range(2)]
    cv = [cv_ref[0, 0, :, hh * NA_DH:(hh + 1) * NA_DH].astype(bf16) for hh in range(2)]
    def scores(r0):
        units = []
        for r in range(r0, r0 + NA_ROW_GROUP):
            rs = _window_row_start(r)
            for hh in range(2):
                q = q_ref[r * GRID_W:(r + 1) * GRID_W, hh * NA_DH:(hh + 1) * NA_DH] * (NA_DH ** -0.5)
                units.append(dict(r=r, hh=hh, d0=rs - r + NA_KR - 1, q=q,
                                  win=slice(rs * GRID_W, (rs + NA_KR) * GRID_W)))
        for un in units:
            hh = un["hh"]
            d0 = un["d0"]
            lane0 = (d0 - d0 % 2) * GRID_W
            un["s_loc"] = (_dot_nt(un["q"], kk[hh][un["win"]])
                           + bias_ref[0, hh, d0 % 2, :, lane0:lane0 + NA_KR * GRID_W])
            un["s_ctx"] = _dot_nt(un["q"], ck[hh])
        return units

    def finish(r0, units):
        for un in units:
            m = jnp.maximum(jnp.max(un["s_loc"], axis=-1, keepdims=True),
                            jnp.max(un["s_ctx"], axis=-1, keepdims=True))
            e_loc = jnp.exp(un["s_loc"] - m)
            e_ctx = jnp.exp(un["s_ctx"] - m)
            un["den"] = jnp.sum(e_loc, axis=-1, keepdims=True) + jnp.sum(e_ctx, axis=-1, keepdims=True)
            un["e_loc"] = e_loc.astype(bf16)
            un["e_ctx"] = e_ctx.astype(bf16)
        for un in units:
            hh = un["hh"]
            un["o"] = (_dot(un["e_loc"], vv[hh][un["win"]]) + _dot(un["e_ctx"], cv[hh])) / un["den"]
        for k in range(NA_ROW_GROUP):
            r = r0 + k
            o_ref[r * GRID_W:(r + 1) * GRID_W, :] = jnp.concatenate(
                [units[2 * k]["o"], units[2 * k + 1]["o"]], axis=1).astype(bf16)

    starts = list(range(0, GRID_ROWS, NA_ROW_GROUP))
    pending = scores(starts[0])
    for g, r0 in enumerate(starts):
        upcoming = scores(starts[g + 1]) if g + 1 < len(starts) else None
        finish(r0, pending)
        pending = upcoming


def _na_attn(qkv16, cache_k, cache_v, bias, layer_slot):
    pw = 2 * NA_DH
    n_pairs = NA_HEADS // 2
    rb0 = N_PROMPT // DEC_SEQ
    return pl.pallas_call(
        _na_attn_kernel,
        grid=(n_pairs, DEC_BATCH),
        in_specs=[
            pl.BlockSpec((DEC_SEQ, pw), lambda hp, b: (rb0 + b, hp)),
            pl.BlockSpec((DEC_SEQ, pw), lambda hp, b: (rb0 + b, n_pairs + hp)),
            pl.BlockSpec((DEC_SEQ, pw), lambda hp, b: (rb0 + b, 2 * n_pairs + hp)),
            pl.BlockSpec((1, 1, PAST_LEN, pw), lambda hp, b: (b, layer_slot, 0, hp)),
            pl.BlockSpec((1, 1, PAST_LEN, pw), lambda hp, b: (b, layer_slot, 0, hp)),
            pl.BlockSpec((1, 2, 2, GRID_W, 2 * NA_KR * GRID_W), lambda hp, b: (layer_slot, hp, 0, 0, 0)),
        ],
        out_specs=pl.BlockSpec((DEC_SEQ, pw), lambda hp, b: (b, hp)),
        out_shape=jax.ShapeDtypeStruct((N_SAMPLE, D_MODEL), bf16),
        compiler_params=_params(("arbitrary", "arbitrary")),
        name="na_attn",
    )(qkv16, qkv16, qkv16, cache_k, cache_v, bias)


def _na_bias_windows(rel_bias):
    n_off = 2 * NA_KR - 1
    col = np.arange(GRID_W)
    col_start = np.clip(col - NA_KC // 2, 0, GRID_W - NA_KC)
    col_mask = (col[None, :] >= col_start[:, None]) & (col[None, :] < col_start[:, None] + NA_KC)
    col_idx = col[None, :] - col[:, None] + NA_KC - 1
    onehot = (col_idx[:, :, None] == np.arange(2 * NA_KC - 1)[None, None, :]) & col_mask[:, :, None]
    g = jnp.einsum("lhji,cki->lhcjk", rel_bias, jnp.asarray(onehot, f32), precision=lax.Precision.HIGHEST)
    g = jnp.where(col_mask[None, None, :, None, :], g, NEG_INF)
    g = g.reshape(g.shape[0], NA_HEADS, GRID_W, n_off * GRID_W)
    pad = (n_off + 1) * GRID_W - g.shape[-1]
    even = jnp.pad(g, ((0, 0), (0, 0), (0, 0), (0, pad)))
    odd = jnp.pad(g[..., GRID_W:], ((0, 0), (0, 0), (0, 0), (0, pad + GRID_W)))
    return jnp.stack([even, odd], axis=2)


def _post_kernel(x_ref, xp_ref, xn_ref, op_ref, opp_ref, opn_ref, os_ref, osp_ref, osn_ref, mod_ref, g_ref,
                 wout_ref, wup_ref, cw_ref, cb_ref, wdn_ref, fg_ref, *rest, final):
    if final:
        yp_ref, ys_ref, xe_s, oe_s, he_s, acc_s = rest
    else:
        out_ref, xe_s, oe_s, he_s, acc_s = rest
    i = pl.program_id(0)
    prompt_tiles = N_PROMPT // TP
    ext = TP + 2 * HALO
    xe_s[0:HALO] = xp_ref[...]
    xe_s[HALO:HALO + TP] = x_ref[...]
    xe_s[HALO + TP:ext] = xn_ref[...]

    @pl.when(i < prompt_tiles)
    def _():
        oe_s[0:HALO] = opp_ref[...]
        oe_s[HALO:HALO + TP] = op_ref[...]
        oe_s[HALO + TP:ext] = opn_ref[...]

    @pl.when(i >= prompt_tiles)
    def _():
        oe_s[0:HALO] = osp_ref[...]
        oe_s[HALO:HALO + TP] = os_ref[...]
        oe_s[HALO + TP:ext] = osn_ref[...]

    m = mod_ref[0, 0]
    x1 = xe_s[...] + m[2:3] * _dot(oe_s[...], wout_ref[0])
    xe_s[...] = x1
    he_s[...] = _norm_mod(x1, g_ref[0], m[3:4], m[4:5]).astype(bf16)

    seq = _seq_len_of_tile(i, TP)

    @pl.when(((i * TP) & (seq - 1)) == 0)
    def _():
        he_s[0:HALO] = jnp.zeros((HALO, D_MODEL), bf16)

    @pl.when((((i + 1) * TP) & (seq - 1)) == 0)
    def _():
        he_s[HALO + TP:ext] = jnp.zeros((HALO, D_MODEL), bf16)

    slab = slice(SEQ - HALO, SEQ + HALO)
    pos = (lax.broadcasted_iota(jnp.int32, (TP, 1), 0)[slab] + i * TP) & (seq - 1)
    first = pos == 0
    last = pos == seq - 1

    def conv(up, c0, width):
        cw = cw_ref[0, :, c0:c0 + width]
        cb = cb_ref[0, :, c0:c0 + width]
        prev = pltpu.roll(up, 1, axis=0)[HALO:HALO + TP]
        mid = up[HALO:HALO + TP]
        nxt = pltpu.roll(up, ext - 1, axis=0)[HALO:HALO + TP]
        u = cw[0:1] * prev + cw[1:2] * mid + cw[2:3] * nxt + cb
        u_slab = (cw[0:1] * jnp.where(first, 0.0, prev[slab]) + cw[1:2] * mid[slab]
                  + cw[2:3] * jnp.where(last, 0.0, nxt[slab]) + cb)
        return jnp.concatenate([u[:slab.start], u_slab, u[slab.stop:]], axis=0)

    chunks = [(c0, min(FF_CHUNK, D_FF - c0)) for c0 in range(0, D_FF, FF_CHUNK)]

    def up_proj(c0, width):
        he = he_s[...]
        return (_dot(he, wup_ref[0, :, c0:c0 + width]), _dot(he, wup_ref[0, :, D_FF + c0:D_FF + c0 + width]))

    pending = up_proj(*chunks[0])
    for idx, (c0, width) in enumerate(chunks):
        upcoming = up_proj(*chunks[idx + 1]) if idx + 1 < len(chunks) else None
        val = conv(pending[0], c0, width)
        gate = conv(pending[1], D_FF + c0, width)
        down = _dot((_silu(gate) * val).astype(bf16), wdn_ref[0, c0:c0 + width, :])
        if idx == 0:
            acc_s[...] = down
        else:
            acc_s[...] += down
        pending = upcoming

    y = xe_s[HALO:HALO + TP] + m[5:6] * acc_s[...]
    if final:
        y = _rms(y) * fg_ref[...]

        @pl.when(i < prompt_tiles)
        def _():
            yp_ref[...] = y

        @pl.when(i >= prompt_tiles)
        def _():
            ys_ref[...] = y
    else:
        out_ref[...] = y


def _post(x, og_p, og_s, mod, norm_g, w_out, w_up, conv_w, conv_b, w_down, final_g, layer, final):
    n_tiles = N_TOK // TP
    prompt_tiles = N_PROMPT // TP
    hb = TP // HALO
    mixer_slot = layer // 2

    def halo_maps(tile_of, n_rows):
        n_hb = n_rows // HALO
        last_tile = n_rows // TP - 1

        def main(i):
            return (jnp.clip(tile_of(i), 0, last_tile), 0)

        def prev(i):
            return (jnp.clip(tile_of(i) * hb - 1, 0, n_hb - 1), 0)

        def nxt(i):
            return (jnp.clip((tile_of(i) + 1) * hb, 0, n_hb - 1), 0)

        return main, prev, nxt

    x_maps = halo_maps(lambda i: i, N_TOK)
    p_maps = halo_maps(lambda i: i, N_PROMPT)
    s_maps = halo_maps(lambda i: i - prompt_tiles, N_SAMPLE)

    def mod_map(i):
        tiles_per_req = DEC_SEQ // TP
        return (layer, jnp.maximum((i - (prompt_tiles - tiles_per_req)) // tiles_per_req, 0), 0, 0)

    def triple(maps):
        return [pl.BlockSpec((TP, D_MODEL), maps[0]), pl.BlockSpec((HALO, D_MODEL), maps[1]),
                pl.BlockSpec((HALO, D_MODEL), maps[2])]

    resident = dict(pipeline_mode=pl.Buffered(1))
    ext = TP + 2 * HALO
    if final:
        out_specs = [pl.BlockSpec((TP, D_MODEL), p_maps[0]), pl.BlockSpec((TP, D_MODEL), s_maps[0])]
        out_shape = [jax.ShapeDtypeStruct((N_PROMPT, D_MODEL), f32), jax.ShapeDtypeStruct((N_SAMPLE, D_MODEL), f32)]
    else:
        out_specs = pl.BlockSpec((TP, D_MODEL), lambda i: (i, 0))
        out_shape = jax.ShapeDtypeStruct((N_TOK, D_MODEL), f32)
    return pl.pallas_call(
        functools.partial(_post_kernel, final=final),
        grid=(n_tiles,),
        in_specs=triple(x_maps) + triple(p_maps) + triple(s_maps) + [
            pl.BlockSpec((1, 1, N_MOD, D_MODEL), mod_map),
            pl.BlockSpec((1, 1, D_MODEL), lambda i: (layer, 0, 0)),
            pl.BlockSpec((1, D_MODEL, D_MODEL), lambda i: (mixer_slot, 0, 0), **resident),
            pl.BlockSpec((1, D_MODEL, 2 * D_FF), lambda i: (layer, 0, 0), **resident),
            pl.BlockSpec((1, 3, 2 * D_FF), lambda i: (layer, 0, 0), **resident),
            pl.BlockSpec((1, 1, 2 * D_FF), lambda i: (layer, 0, 0), **resident),
            pl.BlockSpec((1, D_FF, D_MODEL), lambda i: (layer, 0, 0), **resident),
            pl.BlockSpec((1, D_MODEL), lambda i: (0, 0)),
        ],
        out_specs=out_specs,
        out_shape=out_shape,
        scratch_shapes=[
            pltpu.VMEM((ext, D_MODEL), f32),
            pltpu.VMEM((ext, D_MODEL), bf16),
            pltpu.VMEM((ext, D_MODEL), bf16),
            pltpu.VMEM((TP, D_MODEL), f32),
        ],
        compiler_params=_params(("arbitrary",)),
        name="post",
    )(x, x, x, og_p, og_p, og_p, og_s, og_s, og_s, mod, norm_g, w_out, w_up, conv_w, conv_b, w_down, final_g)


def _gate_lanes(t):
    lead = t.shape[:-2]
    t = t.reshape(lead + (4, N_HEAD_GROUPS, HEAD_GROUP))
    t = jnp.moveaxis(t, -3, -1)
    t = jnp.pad(t, [(0, 0)] * (len(lead) + 2) + [(0, 4)])
    t = t.reshape(lead + (N_HEAD_GROUPS, 8 * HEAD_GROUP))
    t = jnp.pad(t, [(0, 0)] * (len(lead) + 1) + [(0, GATE_LANES - 8 * HEAD_GROUP)])
    return t.reshape(lead + (N_HEAD_GROUPS * GATE_LANES,))


def _gdn_gate_layout(w_in, a_log, dt_bias):
    w_ab = _gate_lanes(w_in[:, 4 * D_MODEL:].reshape(D_MODEL, 4, GDN_HEADS)).astype(bf16)
    zeros = jnp.zeros_like(a_log)
    alog_row = _gate_lanes(jnp.concatenate([a_log, zeros], axis=0))[None, :]
    dtb_row = _gate_lanes(jnp.concatenate([dt_bias, zeros], axis=0))[None, :]
    return w_ab, alog_row.astype(f32), dtb_row.astype(f32)


def kernel(x_prompt, x_sample, state_gdn, cache_k, cache_v, c, c_ctx, w_ada, b_ada, norm1_g, norm2_g,
           gdn_w_in, gdn_conv_w, gdn_a_log, gdn_dt_bias, gdn_norm_g, gdn_w_out,
           na_w_qkv, na_rel_bias, na_w_out, ffn_w_up, ffn_conv_w, ffn_conv_b, ffn_w_down, final_g):
    x = jnp.concatenate([x_prompt.reshape(N_PROMPT, D_MODEL), x_sample.reshape(N_SAMPLE, D_MODEL)], axis=0)
    cvec = jnp.concatenate([c_ctx[None, :], c, jnp.zeros((MOD_ROWS - 1 - DEC_BATCH, D_MODEL), f32)], axis=0)
    mod = _modulation(cvec, w_ada, b_ada).reshape(DEPTH, MOD_ROWS, N_MOD, D_MODEL)
    n1 = norm1_g.reshape(DEPTH, 1, D_MODEL)
    n2 = norm2_g.reshape(DEPTH, 1, D_MODEL)
    fg = final_g.reshape(1, D_MODEL)
    ck = cache_k.reshape(DEC_BATCH, 2, PAST_LEN, D_MODEL)
    cv = cache_v.reshape(DEC_BATCH, 2, PAST_LEN, D_MODEL)
    bias = _na_bias_windows(na_rel_bias)
    prompt_tiles = N_PROMPT // TM
    gdn_w_in16 = gdn_w_in.astype(bf16)
    na_w_qkv16 = na_w_qkv.astype(bf16)
    w_out16 = (gdn_w_out.astype(bf16), na_w_out.astype(bf16))
    ffn_w_up16 = ffn_w_up.astype(bf16)
    ffn_w_down16 = ffn_w_down.astype(bf16)
    ffn_conv_b3 = ffn_conv_b.reshape(DEPTH, 1, 2 * D_FF)

    states, new_ks, new_vs = [], [], []
    for l in range(DEPTH):
        j = l // 2
        if l % 2 == 0:
            w_ab, alog_row, dtb_row = _gdn_gate_layout(gdn_w_in[j], gdn_a_log[j], gdn_dt_bias[j])
            qkvz, gcol, grow = _gdn_in(x, mod, n1, gdn_w_in16, w_ab, gdn_conv_w, alog_row, dtb_row, l)
            ng = gdn_norm_g[j].reshape(1, GDN_DK)
            og_p, st = _gdn_core(qkvz, gcol, grow, ng, seq=SEQ, tile0=0, n_tiles=prompt_tiles, emit_state=True)
            og_s, = _gdn_core(qkvz, gcol, grow, ng, seq=DEC_SEQ, tile0=prompt_tiles, n_tiles=DEC_BATCH,
                              s0=state_gdn, s0_layer=j)
            states.append(st)
        else:
            qkv16, kv32 = _na_in(x, mod, n1, na_w_qkv16, l)
            og_p, nk, nv = _ctx_attn(qkv16, kv32)
            og_s = _na_attn(qkv16, ck, cv, bias, j)
            new_ks.append(nk.reshape(BATCH, SEQ, NA_HEADS, NA_DH))
            new_vs.append(nv.reshape(BATCH, SEQ, NA_HEADS, NA_DH))
        x = _post(x, og_p, og_s, mod, n2, w_out16[l % 2], ffn_w_up16, ffn_conv_w, ffn_conv_b3, ffn_w_down16,
                  fg, l, l == DEPTH - 1)

    y_prompt = x[0].reshape(BATCH, SEQ, D_MODEL)
    y_sample = x[1].reshape(DEC_BATCH, DEC_SEQ, D_MODEL)
    return (y_prompt, y_sample, jnp.stack(states, axis=1), jnp.stack(new_ks, axis=1), jnp.stack(new_vs, axis=1))
```

```python
import functools

import jax
import jax.numpy as jnp
import numpy as np
from jax import lax
from jax.experimental import pallas as pl
from jax.experimental.pallas import tpu as pltpu

f32 = jnp.float32
bf16 = jnp.bfloat16

D_MODEL = 1024
BATCH = 16
SEQ = 256
DEPTH = 4
DEC_BATCH = 8
DEC_SEQ = 1024
PAST_LEN = 256
GRID_W = 64
GRID_ROWS = DEC_SEQ // GRID_W
GDN_HEADS = 8
GDN_DK = 128
GDN_CHUNK = 64
NA_HEADS = 16
NA_DH = 64
NA_KR = 8
NA_KC = 16
D_FF = 2816
N_MOD = 6
EPS = 1e-6
NEG_INF = -1e30

N_PROMPT = BATCH * SEQ
N_SAMPLE = DEC_BATCH * DEC_SEQ
N_TOK = N_PROMPT + N_SAMPLE

TM = 1024
TP = 512
HALO = 16
FF_CHUNK = 512
HEAD_GROUP = 4
N_HEAD_GROUPS = GDN_HEADS // HEAD_GROUP
GATE_LANES = 128
PREP_CHUNKS = 4
NA_ROW_GROUP = 4
MOD_ROWS = 16
MOD_TN = 1536

VMEM_LIMIT = 56 * 1024 * 1024


def _silu(x):
    return x * jax.nn.sigmoid(x)


def _dot(a, b):
    return jnp.dot(a, b, preferred_element_type=f32)


def _dot_nt(a, b):
    return lax.dot_general(a, b, (((1,), (1,)), ((), ())), preferred_element_type=f32)


def _dot_tn(a, b):
    return lax.dot_general(a, b, (((0,), (0,)), ((), ())), preferred_element_type=f32)


def _rms(x):
    return x * lax.rsqrt(jnp.mean(x * x, axis=-1, keepdims=True) + EPS)


def _params(sem, vmem=VMEM_LIMIT):
    return pltpu.CompilerParams(dimension_semantics=sem, vmem_limit_bytes=vmem)


def _mod_kernel(cv_ref, w_ref, b_ref, o_ref):
    s = _silu(cv_ref[...]).astype(bf16)
    o_ref[0] = _dot(s, w_ref[0].astype(bf16)) + b_ref[0]


def _modulation(cvec, w_ada, b_ada):
    n_tiles = (N_MOD * D_MODEL) // MOD_TN
    return pl.pallas_call(
        _mod_kernel,
        grid=(DEPTH, n_tiles),
        in_specs=[
            pl.BlockSpec((MOD_ROWS, D_MODEL), lambda l, n: (0, 0)),
            pl.BlockSpec((1, D_MODEL, MOD_TN), lambda l, n: (l, 0, n)),
            pl.BlockSpec((1, 1, MOD_TN), lambda l, n: (l, 0, n)),
        ],
        out_specs=pl.BlockSpec((1, MOD_ROWS, MOD_TN), lambda l, n: (l, 0, n)),
        out_shape=jax.ShapeDtypeStruct((DEPTH, MOD_ROWS, N_MOD * D_MODEL), f32),
        compiler_params=_params(("arbitrary", "arbitrary")),
        name="modulation",
    )(cvec, w_ada, b_ada.reshape(DEPTH, 1, N_MOD * D_MODEL))


def _mod_row_tm(i):
    return jnp.maximum(i - (N_PROMPT // TM - 1), 0)


def _norm_mod(x, g, shift, scale):
    return (_rms(x) * g) * (1.0 + scale) + shift


def _na_in_kernel(x_ref, mod_ref, g_ref, w_ref, o16_ref, kv_ref):
    i = pl.program_id(0)
    m = mod_ref[0, 0]
    h = _norm_mod(x_ref[...], g_ref[0], m[0:1], m[1:2]).astype(bf16)
    for n in range(3):
        p = _dot(h, w_ref[0, :, n * D_MODEL:(n + 1) * D_MODEL])
        o16_ref[:, n * D_MODEL:(n + 1) * D_MODEL] = p.astype(bf16)
        if n > 0:
            @pl.when(i < N_PROMPT // TM)
            def _():
                kv_ref[:, (n - 1) * D_MODEL:n * D_MODEL] = p


def _na_in(x, mod, norm_g, w, layer):
    n_out = w.shape[2] // D_MODEL
    mixer_slot = layer // 2
    prompt_tiles = N_PROMPT // TM

    return pl.pallas_call(
        _na_in_kernel,
        grid=(N_TOK // TM,),
        in_specs=[
            pl.BlockSpec((TM, D_MODEL), lambda i: (i, 0)),
            pl.BlockSpec((1, 1, N_MOD, D_MODEL), lambda i: (layer, _mod_row_tm(i), 0, 0)),
            pl.BlockSpec((1, 1, D_MODEL), lambda i: (layer, 0, 0)),
            pl.BlockSpec((1, D_MODEL, n_out * D_MODEL), lambda i: (mixer_slot, 0, 0),
                         pipeline_mode=pl.Buffered(1)),
        ],
        out_specs=[pl.BlockSpec((TM, n_out * D_MODEL), lambda i: (i, 0)),
                   pl.BlockSpec((TM, 2 * D_MODEL), lambda i: (jnp.minimum(i, prompt_tiles - 1), 0))],
        out_shape=[jax.ShapeDtypeStruct((N_TOK, w.shape[2]), bf16),
                   jax.ShapeDtypeStruct((N_PROMPT, 2 * D_MODEL), f32)],
        compiler_params=_params(("arbitrary",)),
        name="na_in",
    )(x, mod, norm_g, w)


def _seq_len_of_tile(i, tile):
    return jnp.where(i < N_PROMPT // tile, SEQ, DEC_SEQ)


def _gdn_in_kernel(x_ref, mod_ref, g_ref, w_ref, wab_ref, cw_ref, alog_ref, dtb_ref,
                   o_ref, gcol_ref, grow_ref, h_ref):
    i = pl.program_id(0)
    n = pl.program_id(1)

    @pl.when(n == 0)
    def _():
        m = mod_ref[0, 0]
        h_ref[...] = _norm_mod(x_ref[...], g_ref[0], m[0:1], m[1:2]).astype(bf16)

    def conv_silu():
        p = _dot(h_ref[...], w_ref[0])
        seq = _seq_len_of_tile(i, TM)
        pos = lax.broadcasted_iota(jnp.int32, (TM, 1), 0) & (seq - 1)
        prev = jnp.where(pos == 0, 0.0, pltpu.roll(p, 1, axis=0))
        nxt = jnp.where(pos == seq - 1, 0.0, pltpu.roll(p, TM - 1, axis=0))
        cw = cw_ref[0]
        return _silu(cw[0:1] * prev + cw[1:2] * p + cw[2:3] * nxt)

    @pl.when(n < 2)
    def _():
        s = conv_silu()
        qscale = jnp.where(n == 0, GDN_DK ** -0.5, 1.0).astype(f32)
        for h in range(GDN_HEADS):
            sh = s[:, h * GDN_DK:(h + 1) * GDN_DK]
            nh = sh * lax.rsqrt(jnp.sum(sh * sh, axis=-1, keepdims=True) + EPS)
            o_ref[:, h * GDN_DK:(h + 1) * GDN_DK] = nh * qscale

    @pl.when(n == 2)
    def _():
        o_ref[...] = conv_silu()

    @pl.when(n == 3)
    def _():
        o_ref[...] = _dot(h_ref[...], w_ref[0])

    @pl.when(n == 4)
    def _():
        ab = _dot(h_ref[...], wab_ref[...])
        width = ab.shape[1]
        kind = lax.broadcasted_iota(jnp.int32, (1, width), 1) & 7
        xs = ab + dtb_ref[...]
        softplus = jnp.maximum(xs, 0.0) + jnp.log1p(jnp.exp(-jnp.abs(xs)))
        g = -jnp.exp(alog_ref[...]) * softplus
        pos = lax.broadcasted_iota(jnp.int32, (TM, 1), 0) & (GDN_CHUNK - 1)
        cf = g
        cb = g
        s = 1
        while s < GDN_CHUNK:
            cf = cf + jnp.where(pos >= s, pltpu.roll(cf, s, axis=0), 0.0)
            cb = cb + jnp.where(pos < GDN_CHUNK - s, pltpu.roll(cb, TM - s, axis=0), 0.0)
            s *= 2
        gates = jnp.where(kind == 0, cf, jnp.where(kind == 1, cb, jnp.where(kind < 4, jax.nn.sigmoid(ab), 0.0)))
        gcol_ref[...] = gates
        for c in range(TM // GDN_CHUNK):
            for hg in range(width // GATE_LANES):
                bt = gates[c * GDN_CHUNK:(c + 1) * GDN_CHUNK, hg * GATE_LANES:(hg + 1) * GATE_LANES].T
                grow_ref[c, hg * GATE_LANES:(hg + 1) * GATE_LANES, :] = jnp.concatenate([bt, bt], axis=1)


def _gdn_in(x, mod, norm_g, w_in, w_ab, conv_w, alog_row, dtb_row, layer):
    n_chunks = N_TOK // GDN_CHUNK
    gw = N_HEAD_GROUPS * GATE_LANES
    n_proj = 4
    mixer_slot = layer // 2
    return pl.pallas_call(
        _gdn_in_kernel,
        grid=(N_TOK // TM, n_proj + 1),
        in_specs=[
            pl.BlockSpec((TM, D_MODEL), lambda i, n: (i, 0)),
            pl.BlockSpec((1, 1, N_MOD, D_MODEL), lambda i, n: (layer, _mod_row_tm(i), 0, 0)),
            pl.BlockSpec((1, 1, D_MODEL), lambda i, n: (layer, 0, 0)),
            pl.BlockSpec((1, D_MODEL, D_MODEL), lambda i, n: (mixer_slot, 0, jnp.minimum(n, n_proj - 1))),
            pl.BlockSpec((D_MODEL, gw), lambda i, n: (0, 0)),
            pl.BlockSpec((1, 3, D_MODEL), lambda i, n: (mixer_slot, 0, jnp.minimum(n, 2))),
            pl.BlockSpec((1, gw), lambda i, n: (0, 0)),
            pl.BlockSpec((1, gw), lambda i, n: (0, 0)),
        ],
        out_specs=[
            pl.BlockSpec((TM, D_MODEL), lambda i, n: (i, jnp.minimum(n, n_proj - 1))),
            pl.BlockSpec((TM, gw), lambda i, n: (i, 0)),
            pl.BlockSpec((TM // GDN_CHUNK, gw, 2 * GDN_CHUNK), lambda i, n: (i, 0, 0)),
        ],
        out_shape=[
            jax.ShapeDtypeStruct((N_TOK, n_proj * D_MODEL), f32),
            jax.ShapeDtypeStruct((N_TOK, gw), f32),
            jax.ShapeDtypeStruct((n_chunks, gw, 2 * GDN_CHUNK), f32),
        ],
        scratch_shapes=[pltpu.VMEM((TM, D_MODEL), bf16)],
        compiler_params=_params(("arbitrary", "arbitrary")),
        name="gdn_in",
    )(x, mod, norm_g, w_in, w_ab, conv_w, alog_row, dtb_row)


def _block_diag(x, left):
    return jnp.concatenate([jnp.where(left, x, 0.0), jnp.where(left, 0.0, x)], axis=0).astype(bf16)


def _tri_inverse_pairs(a_list, eye, level_masks, left):
    ts = [eye - jnp.where(level_masks[0], a, 0.0) for a in a_list]
    for mask in level_masks[1:]:
        tls = [_dot(t.astype(bf16), _block_diag(jnp.where(mask, a, 0.0), left)) for t, a in zip(ts, a_list)]
        ts = [t - _dot(tl.astype(bf16), _block_diag(t, left)) for t, tl in zip(ts, tls)]
    return ts


def _gdn_core_kernel(*refs, seq, has_s0, emit_state):
    q_ref, k_ref, v_ref, z_ref, gcol_ref, grow_ref, ng_ref = refs[:7]
    pos = 7
    s0_ref = None
    if has_s0:
        s0_ref = refs[pos]
        pos += 1
    o_ref = refs[pos]
    pos += 1
    sfin_ref = None
    if emit_state:
        sfin_ref = refs[pos]
        pos += 1
    u_s, wq_s, kt_s, in_s, eg_s, st_s, oo_s = refs[pos:]

    C = GDN_CHUNK
    n_seq = TM // seq
    chunks_per_seq = seq // C
    ii = lax.broadcasted_iota(jnp.int32, (C, 2 * C), 0)
    lane = lax.broadcasted_iota(jnp.int32, (C, 2 * C), 1)
    jj = lane & (C - 1)
    left = lane < C
    eye = (ii == jj).astype(f32)
    incl = (left & (ii >= jj)) | (~left & (ii <= jj))
    strict = incl & (ii != jj)
    level_masks = []
    sh = 0
    while (1 << sh) < C:
        level_masks.append(((ii >> (sh + 1)) == (jj >> (sh + 1))) & ((ii >> sh) != (jj >> sh)))
        sh += 1
    zero_rhs = jnp.zeros((C, 2 * GDN_DK), bf16)

    def prep(it, carry):
        units = []
        for cc in range(PREP_CHUNKS):
            c = it * PREP_CHUNKS + cc
            rows = pl.ds(pl.multiple_of(c * C, C), C)
            gc_tile = gcol_ref[rows, :]
            gr_tile = grow_ref[c]
            for hl in range(HEAD_GROUP):
                ls = slice(hl * GDN_DK, (hl + 1) * GDN_DK)
                kc = k_ref[rows, ls]
                qc = q_ref[rows, ls]
                k16 = kc.astype(bf16)
                kk16 = jnp.concatenate([k16, k16], axis=0)
                l0 = 8 * hl
                units.append(dict(
                    c=c, rows=rows, hl=hl, kc=kc, qc=qc, vc=v_ref[rows, ls],
                    gram=_dot_nt(k16, kk16), qk=_dot_nt(qc.astype(bf16), kk16),
                    gcol=(gc_tile[:, l0:l0 + 1], gc_tile[:, l0 + 1:l0 + 2]),
                    bcol=(gc_tile[:, l0 + 2:l0 + 3], gc_tile[:, l0 + 3:l0 + 4]),
                    grow=jnp.where(left[0:1], gr_tile[l0:l0 + 1, :], gr_tile[l0 + 1:l0 + 2, :])))
        for un in units:
            diff = jnp.where(left, un["gcol"][0], un["gcol"][1]) - un["grow"]
            un["decay"] = jnp.where(incl, jnp.exp(jnp.where(incl, diff, 0.0)), 0.0)
            bpair = jnp.where(left, un["bcol"][0], un["bcol"][1])
            un["a"] = jnp.where(strict, un["gram"] * un["decay"] * bpair, 0.0)
        ts = _tri_inverse_pairs([un["a"] for un in units], eye, level_masks, left)
        for un, t in zip(units, ts):
            un["eg"] = [jnp.exp(g) for g in un["gcol"]]
            halves = []
            for d in range(2):
                b = un["bcol"][d]
                halves.append(jnp.concatenate([(un["vc"] * b).astype(bf16),
                                               (un["kc"] * (b * un["eg"][d])).astype(bf16)], axis=1))
            rhs = jnp.concatenate([jnp.concatenate([halves[0], zero_rhs], axis=1),
                                   jnp.concatenate([zero_rhs, halves[1]], axis=1)], axis=0)
            un["uw"] = _dot(t.astype(bf16), rhs)
        for un in units:
            c = un["c"]
            rows = un["rows"]
            intra = un["qk"] * un["decay"]
            for d in range(2):
                idx = un["hl"] * 2 + d
                gcol = un["gcol"][d]
                glast = gcol[C - 1:C] if d == 0 else gcol[0:1]
                base = 2 * d * GDN_DK
                u_s[idx, rows, :] = un["uw"][:, base:base + GDN_DK]
                wq_s[idx, c, 0:C, :] = un["uw"][:, base + GDN_DK:base + 2 * GDN_DK].astype(bf16)
                wq_s[idx, c, C:2 * C, :] = (un["qc"] * un["eg"][d]).astype(bf16)
                kt_s[idx, rows, :] = (un["kc"] * jnp.exp(glast - gcol)).astype(bf16)
                in_s[idx, c] = jnp.where(left if d == 0 else ~left, intra, 0.0).astype(bf16)
                eg_s[idx, c] = jnp.broadcast_to(jnp.exp(glast), (8, GDN_DK))
        return carry

    lax.fori_loop(0, TM // C // PREP_CHUNKS, prep, 0)

    lanes = []
    for si in range(n_seq):
        for hl in range(HEAD_GROUP):
            for d in range(2):
                lanes.append((si, hl, d))
    for li, (si, hl, d) in enumerate(lanes):
        st_s[li] = s0_ref[si, 0, d, hl] if has_s0 else jnp.zeros((GDN_DK, GDN_DK), f32)

    def scan(t, carry):
        work = []
        for li, (si, hl, d) in enumerate(lanes):
            c = si * chunks_per_seq + (t if d == 0 else chunks_per_seq - 1 - t)
            rows = pl.ds(pl.multiple_of(c * C, C), C)
            work.append(dict(li=li, idx=hl * 2 + d, hl=hl, d=d, c=c, rows=rows, s=st_s[li]))
        for w in work:
            w["wsqs"] = _dot(wq_s[w["idx"], w["c"]], w["s"].astype(bf16))
        for w in work:
            v16 = (u_s[w["idx"], w["rows"], :] - w["wsqs"][:C]).astype(bf16)
            w["o"] = w["wsqs"][C:] + _dot(in_s[w["idx"], w["c"]], jnp.concatenate([v16, v16], axis=0))
            w["ds"] = _dot_tn(kt_s[w["idx"], w["rows"], :], v16)
        for w in work:
            ls = slice(w["hl"] * GDN_DK, (w["hl"] + 1) * GDN_DK)
            oo_s[w["d"], w["rows"], ls] = w["o"]
            st_s[w["li"]] = w["s"] * eg_s[w["idx"], w["c"]][0:1, :] + w["ds"]
        return carry

    lax.fori_loop(0, chunks_per_seq, scan, 0)

    ng = ng_ref[...]
    for hl in range(HEAD_GROUP):
        ls = slice(hl * GDN_DK, (hl + 1) * GDN_DK)
        o = oo_s[0, :, ls] + oo_s[1, :, ls]
        o_ref[:, ls] = (_rms(o) * ng * _silu(z_ref[:, ls])).astype(bf16)
    if emit_state:
        for li, (si, hl, d) in enumerate(lanes):
            sfin_ref[si, d, hl] = st_s[li]


def _gdn_core(qkvz, gcol, grow, norm_g_row, *, seq, tile0, n_tiles, s0=None, s0_layer=0, emit_state=False):
    hw = HEAD_GROUP * GDN_DK
    n_seq = TM // seq
    n_chunks = TM // GDN_CHUNK
    n_ch = HEAD_GROUP * 2

    def col_spec(base):
        return pl.BlockSpec((TM, hw), lambda s, hg: (tile0 + s, base * N_HEAD_GROUPS + hg))

    in_specs = [col_spec(0), col_spec(1), col_spec(2), col_spec(3),
                pl.BlockSpec((TM, GATE_LANES), lambda s, hg: (tile0 + s, hg)),
                pl.BlockSpec((n_chunks, GATE_LANES, 2 * GDN_CHUNK), lambda s, hg: (tile0 + s, hg, 0)),
                pl.BlockSpec((1, GDN_DK), lambda s, hg: (0, 0))]
    args = [qkvz, qkvz, qkvz, qkvz, gcol, grow, norm_g_row]
    if s0 is not None:
        in_specs.append(pl.BlockSpec((n_seq, 1, 2, HEAD_GROUP, GDN_DK, GDN_DK),
                                     lambda s, hg: (s, s0_layer, 0, hg, 0, 0)))
        args.append(s0)
    out_specs = [pl.BlockSpec((TM, hw), lambda s, hg: (s, hg))]
    out_shape = [jax.ShapeDtypeStruct((n_tiles * TM, D_MODEL), bf16)]
    if emit_state:
        out_specs.append(pl.BlockSpec((n_seq, 2, HEAD_GROUP, GDN_DK, GDN_DK), lambda s, hg: (s, 0, hg, 0, 0)))
        out_shape.append(jax.ShapeDtypeStruct((n_tiles * n_seq, 2, GDN_HEADS, GDN_DK, GDN_DK), f32))

    return pl.pallas_call(
        functools.partial(_gdn_core_kernel, seq=seq, has_s0=s0 is not None, emit_state=emit_state),
        grid=(n_tiles, N_HEAD_GROUPS),
        in_specs=in_specs,
        out_specs=out_specs,
        out_shape=out_shape,
        scratch_shapes=[
            pltpu.VMEM((n_ch, TM, GDN_DK), f32),
            pltpu.VMEM((n_ch, n_chunks, 2 * GDN_CHUNK, GDN_DK), bf16),
            pltpu.VMEM((n_ch, TM, GDN_DK), bf16),
            pltpu.VMEM((n_ch, n_chunks, GDN_CHUNK, 2 * GDN_CHUNK), bf16),
            pltpu.VMEM((n_ch, n_chunks, 8, GDN_DK), f32),
            pltpu.VMEM((n_seq * n_ch, GDN_DK, GDN_DK), f32),
            pltpu.VMEM((2, TM, hw), f32),
        ],
        compiler_params=_params(("arbitrary", "arbitrary")),
        name="gdn_core_%d" % seq,
    )(*args)


def _ctx_attn_kernel(q_ref, k_ref, v_ref, o_ref, ko_ref, vo_ref):
    k = k_ref[...]
    v = v_ref[...]
    ko_ref[...] = k
    vo_ref[...] = v
    heads = []
    for h in range(NA_HEADS):
        hs = slice(h * NA_DH, (h + 1) * NA_DH)
        heads.append(dict(s=_dot_nt(q_ref[:, hs] * (NA_DH ** -0.5), k[:, hs].astype(bf16)),
                          v16=v[:, hs].astype(bf16)))
    for hd in heads:
        e = jnp.exp(hd["s"] - jnp.max(hd["s"], axis=-1, keepdims=True))
        hd["p16"] = (e / jnp.sum(e, axis=-1, keepdims=True)).astype(bf16)
    outs = [_dot(hd["p16"], hd["v16"]) for hd in heads]
    for hp in range(NA_HEADS // 2):
        o_ref[:, 2 * hp * NA_DH:(2 * hp + 2) * NA_DH] = jnp.concatenate(
            outs[2 * hp:2 * hp + 2], axis=1).astype(bf16)


def _ctx_attn(qkv16, kv32):
    cache_shape = jax.ShapeDtypeStruct((N_PROMPT, D_MODEL), f32)
    row_spec = pl.BlockSpec((SEQ, D_MODEL), lambda b: (b, 0))
    return pl.pallas_call(
        _ctx_attn_kernel,
        grid=(BATCH,),
        in_specs=[row_spec, row_spec, pl.BlockSpec((SEQ, D_MODEL), lambda b: (b, 1))],
        out_specs=[row_spec, row_spec, row_spec],
        out_shape=[jax.ShapeDtypeStruct((N_PROMPT, D_MODEL), bf16), cache_shape, cache_shape],
        compiler_params=_params(("arbitrary",)),
        name="ctx_attn",
    )(qkv16, kv32, kv32)


def _window_row_start(r):
    return min(max(r - NA_KR // 2, 0), GRID_ROWS - NA_KR)


def _na_attn_kernel(q_ref, k_ref, v_ref, ck_ref, cv_ref, bias_ref, o_ref):
    kk = [k_ref[:, hh * NA_DH:(hh + 1) * NA_DH] for hh in range(2)]
    vv = [v_ref[:, hh * NA_DH:(hh + 1) * NA_DH] for hh in range(2)]
    ck = [ck_ref[0, 0, :, hh * NA_DH:(hh + 1) * NA_DH].astype(bf16) for hh in range(2)]
    cv = [cv_ref[0, 0, :, hh * NA_DH:(hh + 1) * NA_DH].astype(bf16) for hh in range(2)]
    def scores(r0):
        units = []
        for r in range(r0, r0 + NA_ROW_GROUP):
            rs = _window_row_start(r)
            for hh in range(2):
                q = q_ref[r * GRID_W:(r + 1) * GRID_W, hh * NA_DH:(hh + 1) * NA_DH] * (NA_DH ** -0.5)
                units.append(dict(r=r, hh=hh, d0=rs - r + NA_KR - 1, q=q,
                                  win=slice(rs * GRID_W, (rs + NA_KR) * GRID_W)))
        for un in units:
            hh = un["hh"]
            d0 = un["d0"]
            lane0 = (d0 - d0 % 2) * GRID_W
            un["s_loc"] = (_dot_nt(un["q"], kk[hh][un["win"]])
                           + bias_ref[0, hh, d0 % 2, :, lane0:lane0 + NA_KR * GRID_W])
            un["s_ctx"] = _dot_nt(un["q"], ck[hh])
        return units

    def finish(r0, units):
        for un in units:
            m = jnp.maximum(jnp.max(un["s_loc"], axis=-1, keepdims=True),
                            jnp.max(un["s_ctx"], axis=-1, keepdims=True))
            e_loc = jnp.exp(un["s_loc"] - m)
            e_ctx = jnp.exp(un["s_ctx"] - m)
            un["den"] = jnp.sum(e_loc, axis=-1, keepdims=True) + jnp.sum(e_ctx, axis=-1, keepdims=True)
            un["e_loc"] = e_loc.astype(bf16)
            un["e_ctx"] = e_ctx.astype(bf16)
        for un in units:
            hh = un["hh"]
            un["o"] = (_dot(un["e_loc"], vv[hh][un["win"]]) + _dot(un["e_ctx"], cv[hh])) / un["den"]
        for k in range(NA_ROW_GROUP):
            r = r0 + k
            o_ref[r * GRID_W:(r + 1) * GRID_W, :] = jnp.concatenate(
                [units[2 * k]["o"], units[2 * k + 1]["o"]], axis=1).astype(bf16)

    starts = list(range(0, GRID_ROWS, NA_ROW_GROUP))
    pending = scores(starts[0])
    for g, r0 in enumerate(starts):
        upcoming = scores(starts[g + 1]) if g + 1 < len(starts) else None
        finish(r0, pending)
        pending = upcoming


def _na_attn(qkv16, cache_k, cache_v, bias, layer_slot):
    pw = 2 * NA_DH
    n_pairs = NA_HEADS // 2
    rb0 = N_PROMPT // DEC_SEQ
    return pl.pallas_call(
        _na_attn_kernel,
        grid=(n_pairs, DEC_BATCH),
        in_specs=[
            pl.BlockSpec((DEC_SEQ, pw), lambda hp, b: (rb0 + b, hp)),
            pl.BlockSpec((DEC_SEQ, pw), lambda hp, b: (rb0 + b, n_pairs + hp)),
            pl.BlockSpec((DEC_SEQ, pw), lambda hp, b: (rb0 + b, 2 * n_pairs + hp)),
            pl.BlockSpec((1, 1, PAST_LEN, pw), lambda hp, b: (b, layer_slot, 0, hp)),
            pl.BlockSpec((1, 1, PAST_LEN, pw), lambda hp, b: (b, layer_slot, 0, hp)),
            pl.BlockSpec((1, 2, 2, GRID_W, 2 * NA_KR * GRID_W), lambda hp, b: (layer_slot, hp, 0, 0, 0)),
        ],
        out_specs=pl.BlockSpec((DEC_SEQ, pw), lambda hp, b: (b, hp)),
        out_shape=jax.ShapeDtypeStruct((N_SAMPLE, D_MODEL), bf16),
        compiler_params=_params(("arbitrary", "arbitrary")),
        name="na_attn",
    )(qkv16, qkv16, qkv16, cache_k, cache_v, bias)


def _na_bias_windows(rel_bias):
    n_off = 2 * NA_KR - 1
    col = np.arange(GRID_W)
    col_start = np.clip(col - NA_KC // 2, 0, GRID_W - NA_KC)
    col_mask = (col[None, :] >= col_start[:, None]) & (col[None, :] < col_start[:, None] + NA_KC)
    col_idx = col[None, :] - col[:, None] + NA_KC - 1
    onehot = (col_idx[:, :, None] == np.arange(2 * NA_KC - 1)[None, None, :]) & col_mask[:, :, None]
    g = jnp.einsum("lhji,cki->lhcjk", rel_bias, jnp.asarray(onehot, f32), precision=lax.Precision.HIGHEST)
    g = jnp.where(col_mask[None, None, :, None, :], g, NEG_INF)
    g = g.reshape(g.shape[0], NA_HEADS, GRID_W, n_off * GRID_W)
    pad = (n_off + 1) * GRID_W - g.shape[-1]
    even = jnp.pad(g, ((0, 0), (0, 0), (0, 0), (0, pad)))
    odd = jnp.pad(g[..., GRID_W:], ((0, 0), (0, 0), (0, 0), (0, pad + GRID_W)))
    return jnp.stack([even, odd], axis=2)


def _post_kernel(x_ref, xp_ref, xn_ref, op_ref, opp_ref, opn_ref, os_ref, osp_ref, osn_ref, mod_ref, g_ref,
                 wout_ref, wup_ref, cw_ref, cb_ref, wdn_ref, fg_ref, *rest, final):
    if final:
        yp_ref, ys_ref, xe_s, oe_s, he_s, acc_s = rest
    else:
        out_ref, xe_s, oe_s, he_s, acc_s = rest
    i = pl.program_id(0)
    prompt_tiles = N_PROMPT // TP
    ext = TP + 2 * HALO
    xe_s[0:HALO] = xp_ref[...]
    xe_s[HALO:HALO + TP] = x_ref[...]
    xe_s[HALO + TP:ext] = xn_ref[...]

    @pl.when(i < prompt_tiles)
    def _():
        oe_s[0:HALO] = opp_ref[...]
        oe_s[HALO:HALO + TP] = op_ref[...]
        oe_s[HALO + TP:ext] = opn_ref[...]

    @pl.when(i >= prompt_tiles)
    def _():
        oe_s[0:HALO] = osp_ref[...]
        oe_s[HALO:HALO + TP] = os_ref[...]
        oe_s[HALO + TP:ext] = osn_ref[...]

    m = mod_ref[0, 0]
    x1 = xe_s[...] + m[2:3] * _dot(oe_s[...], wout_ref[0])
    xe_s[...] = x1
    he_s[...] = _norm_mod(x1, g_ref[0], m[3:4], m[4:5]).astype(bf16)

    seq = _seq_len_of_tile(i, TP)

    @pl.when(((i * TP) & (seq - 1)) == 0)
    def _():
        he_s[0:HALO] = jnp.zeros((HALO, D_MODEL), bf16)

    @pl.when((((i + 1) * TP) & (seq - 1)) == 0)
    def _():
        he_s[HALO + TP:ext] = jnp.zeros((HALO, D_MODEL), bf16)

    slab = slice(SEQ - HALO, SEQ + HALO)
    pos = (lax.broadcasted_iota(jnp.int32, (TP, 1), 0)[slab] + i * TP) & (seq - 1)
    first = pos == 0
    last = pos == seq - 1

    def conv(up, c0, width):
        cw = cw_ref[0, :, c0:c0 + width]
        cb = cb_ref[0, :, c0:c0 + width]
        prev = pltpu.roll(up, 1, axis=0)[HALO:HALO + TP]
        mid = up[HALO:HALO + TP]
        nxt = pltpu.roll(up, ext - 1, axis=0)[HALO:HALO + TP]
        u = cw[0:1] * prev + cw[1:2] * mid + cw[2:3] * nxt + cb
        u_slab = (cw[0:1] * jnp.where(first, 0.0, prev[slab]) + cw[1:2] * mid[slab]
                  + cw[2:3] * jnp.where(last, 0.0, nxt[slab]) + cb)
        return jnp.concatenate([u[:slab.start], u_slab, u[slab.stop:]], axis=0)

    chunks = [(c0, min(FF_CHUNK, D_FF - c0)) for c0 in range(0, D_FF, FF_CHUNK)]

    def up_proj(c0, width):
        he = he_s[...]
        return (_dot(he, wup_ref[0, :, c0:c0 + width]), _dot(he, wup_ref[0, :, D_FF + c0:D_FF + c0 + width]))

    pending = up_proj(*chunks[0])
    for idx, (c0, width) in enumerate(chunks):
        upcoming = up_proj(*chunks[idx + 1]) if idx + 1 < len(chunks) else None
        val = conv(pending[0], c0, width)
        gate = conv(pending[1], D_FF + c0, width)
        down = _dot((_silu(gate) * val).astype(bf16), wdn_ref[0, c0:c0 + width, :])
        if idx == 0:
            acc_s[...] = down
        else:
            acc_s[...] += down
        pending = upcoming

    y = xe_s[HALO:HALO + TP] + m[5:6] * acc_s[...]
    if final:
        y = _rms(y) * fg_ref[...]

        @pl.when(i < prompt_tiles)
        def _():
            yp_ref[...] = y

        @pl.when(i >= prompt_tiles)
        def _():
            ys_ref[...] = y
    else:
        out_ref[...] = y


def _post(x, og_p, og_s, mod, norm_g, w_out, w_up, conv_w, conv_b, w_down, final_g, layer, final):
    n_tiles = N_TOK // TP
    prompt_tiles = N_PROMPT // TP
    hb = TP // HALO
    mixer_slot = layer // 2

    def halo_maps(tile_of, n_rows):
        n_hb = n_rows // HALO
        last_tile = n_rows // TP - 1

        def main(i):
            return (jnp.clip(tile_of(i), 0, last_tile), 0)

        def prev(i):
            return (jnp.clip(tile_of(i) * hb - 1, 0, n_hb - 1), 0)

        def nxt(i):
            return (jnp.clip((tile_of(i) + 1) * hb, 0, n_hb - 1), 0)

        return main, prev, nxt

    x_maps = halo_maps(lambda i: i, N_TOK)
    p_maps = halo_maps(lambda i: i, N_PROMPT)
    s_maps = halo_maps(lambda i: i - prompt_tiles, N_SAMPLE)

    def mod_map(i):
        tiles_per_req = DEC_SEQ // TP
        return (layer, jnp.maximum((i - (prompt_tiles - tiles_per_req)) // tiles_per_req, 0), 0, 0)

    def triple(maps):
        return [pl.BlockSpec((TP, D_MODEL), maps[0]), pl.BlockSpec((HALO, D_MODEL), maps[1]),
                pl.BlockSpec((HALO, D_MODEL), maps[2])]

    resident = dict(pipeline_mode=pl.Buffered(1))
    ext = TP + 2 * HALO
    if final:
        out_specs = [pl.BlockSpec((TP, D_MODEL), p_maps[0]), pl.BlockSpec((TP, D_MODEL), s_maps[0])]
        out_shape = [jax.ShapeDtypeStruct((N_PROMPT, D_MODEL), f32), jax.ShapeDtypeStruct((N_SAMPLE, D_MODEL), f32)]
    else:
        out_specs = pl.BlockSpec((TP, D_MODEL), lambda i: (i, 0))
        out_shape = jax.ShapeDtypeStruct((N_TOK, D_MODEL), f32)
    return pl.pallas_call(
        functools.partial(_post_kernel, final=final),
        grid=(n_tiles,),
        in_specs=triple(x_maps) + triple(p_maps) + triple(s_maps) + [
            pl.BlockSpec((1, 1, N_MOD, D_MODEL), mod_map),
            pl.BlockSpec((1, 1, D_MODEL), lambda i: (layer, 0, 0)),
            pl.BlockSpec((1, D_MODEL, D_MODEL), lambda i: (mixer_slot, 0, 0), **resident),
            pl.BlockSpec((1, D_MODEL, 2 * D_FF), lambda i: (layer, 0, 0), **resident),
            pl.BlockSpec((1, 3, 2 * D_FF), lambda i: (layer, 0, 0), **resident),
            pl.BlockSpec((1, 1, 2 * D_FF), lambda i: (layer, 0, 0), **resident),
            pl.BlockSpec((1, D_FF, D_MODEL), lambda i: (layer, 0, 0), **resident),
            pl.BlockSpec((1, D_MODEL), lambda i: (0, 0)),
        ],
        out_specs=out_specs,
        out_shape=out_shape,
        scratch_shapes=[
            pltpu.VMEM((ext, D_MODEL), f32),
            pltpu.VMEM((ext, D_MODEL), bf16),
            pltpu.VMEM((ext, D_MODEL), bf16),
            pltpu.VMEM((TP, D_MODEL), f32),
        ],
        compiler_params=_params(("arbitrary",)),
        name="post",
    )(x, x, x, og_p, og_p, og_p, og_s, og_s, og_s, mod, norm_g, w_out, w_up, conv_w, conv_b, w_down, final_g)


def _gate_lanes(t):
    lead = t.shape[:-2]
    t = t.reshape(lead + (4, N_HEAD_GROUPS, HEAD_GROUP))
    t = jnp.moveaxis(t, -3, -1)
    t = jnp.pad(t, [(0, 0)] * (len(lead) + 2) + [(0, 4)])
    t = t.reshape(lead + (N_HEAD_GROUPS, 8 * HEAD_GROUP))
    t = jnp.pad(t, [(0, 0)] * (len(lead) + 1) + [(0, GATE_LANES - 8 * HEAD_GROUP)])
    return t.reshape(lead + (N_HEAD_GROUPS * GATE_LANES,))


def _gdn_gate_layout(w_in, a_log, dt_bias):
    w_ab = _gate_lanes(w_in[:, 4 * D_MODEL:].reshape(D_MODEL, 4, GDN_HEADS)).astype(bf16)
    zeros = jnp.zeros_like(a_log)
    alog_row = _gate_lanes(jnp.concatenate([a_log, zeros], axis=0))[None, :]
    dtb_row = _gate_lanes(jnp.concatenate([dt_bias, zeros], axis=0))[None, :]
    return w_ab, alog_row.astype(f32), dtb_row.astype(f32)


def kernel(x_prompt, x_sample, state_gdn, cache_k, cache_v, c, c_ctx, w_ada, b_ada, norm1_g, norm2_g,
           gdn_w_in, gdn_conv_w, gdn_a_log, gdn_dt_bias, gdn_norm_g, gdn_w_out,
           na_w_qkv, na_rel_bias, na_w_out, ffn_w_up, ffn_conv_w, ffn_conv_b, ffn_w_down, final_g):
    x = jnp.concatenate([x_prompt.reshape(N_PROMPT, D_MODEL), x_sample.reshape(N_SAMPLE, D_MODEL)], axis=0)
    cvec = jnp.concatenate([c_ctx[None, :], c, jnp.zeros((MOD_ROWS - 1 - DEC_BATCH, D_MODEL), f32)], axis=0)
    mod = _modulation(cvec, w_ada, b_ada).reshape(DEPTH, MOD_ROWS, N_MOD, D_MODEL)
    n1 = norm1_g.reshape(DEPTH, 1, D_MODEL)
    n2 = norm2_g.reshape(DEPTH, 1, D_MODEL)
    fg = final_g.reshape(1, D_MODEL)
    ck = cache_k.reshape(DEC_BATCH, 2, PAST_LEN, D_MODEL)
    cv = cache_v.reshape(DEC_BATCH, 2, PAST_LEN, D_MODEL)
    bias = _na_bias_windows(na_rel_bias)
    prompt_tiles = N_PROMPT // TM
    gdn_w_in16 = gdn_w_in.astype(bf16)
    na_w_qkv16 = na_w_qkv.astype(bf16)
    w_out16 = (gdn_w_out.astype(bf16), na_w_out.astype(bf16))
    ffn_w_up16 = ffn_w_up.astype(bf16)
    ffn_w_down16 = ffn_w_down.astype(bf16)
    ffn_conv_b3 = ffn_conv_b.reshape(DEPTH, 1, 2 * D_FF)

    states, new_ks, new_vs = [], [], []
    for l in range(DEPTH):
        j = l // 2
        if l % 2 == 0:
            w_ab, alog_row, dtb_row = _gdn_gate_layout(gdn_w_in[j], gdn_a_log[j], gdn_dt_bias[j])
            qkvz, gcol, grow = _gdn_in(x, mod, n1, gdn_w_in16, w_ab, gdn_conv_w, alog_row, dtb_row, l)
            ng = gdn_norm_g[j].reshape(1, GDN_DK)
            og_p, st = _gdn_core(qkvz, gcol, grow, ng, seq=SEQ, tile0=0, n_tiles=prompt_tiles, emit_state=True)
            og_s, = _gdn_core(qkvz, gcol, grow, ng, seq=DEC_SEQ, tile0=prompt_tiles, n_tiles=DEC_BATCH,
                              s0=state_gdn, s0_layer=j)
            states.append(st)
        else:
            qkv16, kv32 = _na_in(x, mod, n1, na_w_qkv16, l)
            og_p, nk, nv = _ctx_attn(qkv16, kv32)
            og_s = _na_attn(qkv16, ck, cv, bias, j)
            new_ks.append(nk.reshape(BATCH, SEQ, NA_HEADS, NA_DH))
            new_vs.append(nv.reshape(BATCH, SEQ, NA_HEADS, NA_DH))
        x = _post(x, og_p, og_s, mod, n2, w_out16[l % 2], ffn_w_up16, ffn_conv_w, ffn_conv_b3, ffn_w_down16,
                  fg, l, l == DEPTH - 1)

    y_prompt = x[0].reshape(BATCH, SEQ, D_MODEL)
    y_sample = x[1].reshape(DEC_BATCH, DEC_SEQ, D_MODEL)
    return (y_prompt, y_sample, jnp.stack(states, axis=1), jnp.stack(new_ks, axis=1), jnp.stack(new_vs, axis=1))
```
